```python
import jax, jax.numpy as jnp
from jax import lax
import numpy as np

D_MODEL = 2048
BATCH = 1
SEQ = 8192
DEPTH = 2
DEC_BATCH = 4
DEC_SEQ = 8192
PAST_LEN = 128

MIX_WIDTH = D_MODEL
HEAD_DIM = 128
ATTN_WIDTH = MIX_WIDTH // 2
N_Q_HEADS = ATTN_WIDTH // HEAD_DIM
N_KV_HEADS = 2
Q_PER_KV = N_Q_HEADS // N_KV_HEADS
KV_WIDTH = N_KV_HEADS * HEAD_DIM
FOURIER_WIDTH = MIX_WIDTH - ATTN_WIDTH
N_FOURIER_GROUPS = 8
FOURIER_GROUP_DIM = FOURIER_WIDTH // N_FOURIER_GROUPS
IN_PROJ_WIDTH = ATTN_WIDTH + 2 * KV_WIDTH + FOURIER_WIDTH
WINDOW = 128
BLOCK = 128
ROPE_THETA = 10000.0
D_FF = 5632
N_SUBLAYERS = 3
N_MOD = 3
RMS_EPS = 1e-6
MOD_SCALE = 0.1
NEG_INF = -1e30

kernel_name = "hymba_style_fnet_swa_macaron_encoder"


def rms_norm(x, g):
    xf = x.astype(jnp.float32)
    y = xf * lax.rsqrt(jnp.mean(xf * xf, axis=-1, keepdims=True) + RMS_EPS)
    return (y * g.astype(jnp.float32)).astype(x.dtype)


def rope_tables(seq_len):
    inv_freq = ROPE_THETA ** (-jnp.arange(0, HEAD_DIM, 2, dtype=jnp.float32) / HEAD_DIM)
    ang = jnp.arange(seq_len, dtype=jnp.float32)[:, None] * inv_freq[None, :]
    return jnp.cos(ang), jnp.sin(ang)


def apply_rope(t, cos, sin):
    tf = t.astype(jnp.float32)
    t1, t2 = tf[..., : HEAD_DIM // 2], tf[..., HEAD_DIM // 2:]
    c = cos[None, :, None, :]
    s = sin[None, :, None, :]
    return jnp.concatenate([t1 * c - t2 * s, t1 * s + t2 * c], axis=-1).astype(t.dtype)


def band_mask(seq_len):
    nb = seq_len // BLOCK
    n = jnp.arange(nb)[:, None, None]
    i = jnp.arange(BLOCK)[None, :, None]
    j = jnp.arange(3 * BLOCK)[None, None, :]
    qpos = n * BLOCK + i
    kpos = (n - 1) * BLOCK + j
    return (jnp.abs(qpos - kpos) <= WINDOW) & (kpos >= 0) & (kpos < seq_len)


def windowed_gqa_attention(q, k, v, sink, cos, sin, mask):
    B, S, _ = q.shape
    nb = S // BLOCK
    q = apply_rope(q.reshape(B, S, N_Q_HEADS, HEAD_DIM), cos, sin)
    k = apply_rope(k.reshape(B, S, N_KV_HEADS, HEAD_DIM), cos, sin)
    v = v.reshape(B, S, N_KV_HEADS, HEAD_DIM)
    qb = q.reshape(B, nb, BLOCK, N_KV_HEADS, Q_PER_KV, HEAD_DIM)

    def band(t):
        tp = jnp.pad(t, ((0, 0), (BLOCK, BLOCK), (0, 0), (0, 0)))
        tp = tp.reshape(B, nb + 2, BLOCK, N_KV_HEADS, HEAD_DIM)
        return jnp.concatenate([tp[:, :-2], tp[:, 1:-1], tp[:, 2:]], axis=2)

    kb, vb = band(k), band(v)
    scores = jnp.einsum('bnqhgd,bnkhd->bnhgqk', qb, kb,
                        preferred_element_type=jnp.float32) * (HEAD_DIM ** -0.5)
    scores = jnp.where(mask[None, :, None, None], scores, NEG_INF)
    sink_l = sink.astype(jnp.float32).reshape(1, 1, N_KV_HEADS, Q_PER_KV, 1, 1)
    m = jnp.maximum(jnp.max(scores, axis=-1, keepdims=True), sink_l)
    p = jnp.exp(scores - m)
    probs = p / (jnp.sum(p, axis=-1, keepdims=True) + jnp.exp(sink_l - m))
    out = jnp.einsum('bnhgqk,bnkhd->bnqhgd', probs.astype(v.dtype), vb)
    return out.reshape(B, S, ATTN_WIDTH)


def fourier_mix(u, w_lin):
    B, S, _ = u.shape
    ug = u.reshape(B, S, N_FOURIER_GROUPS, FOURIER_GROUP_DIM).astype(jnp.float32)
    f = jnp.fft.fft2(ug, axes=(1, 3), norm='ortho').real.astype(u.dtype)
    out = jnp.einsum('bsgc,gce->bsge', f, w_lin)
    return out.reshape(B, S, FOURIER_WIDTH)


def swiglu(h, w_gate, w_up, w_down):
    return (jax.nn.silu(h @ w_gate) * (h @ w_up)) @ w_down


def encoder_trunk(x, c, w_mod, b_mod, pre_g, post_g, ffn_w_gate, ffn_w_up, ffn_w_down,
                  w_in, attn_sink, fourier_w, branch_g, w_out):
    B, S, D = x.shape
    cos, sin = rope_tables(S)
    mask = band_mask(S)
    c_act = jax.nn.silu(c)
    for l in range(DEPTH):
        mod = (c_act @ w_mod[l] + b_mod[l]).reshape(B, N_SUBLAYERS, N_MOD, D)

        def pre(xx, j):
            shift = mod[:, j, 0][:, None, :]
            scale = mod[:, j, 1][:, None, :]
            return rms_norm(xx, pre_g[l, j]) * (1.0 + scale) + shift

        def post(xx, y, j, weight):
            gate = mod[:, j, 2][:, None, :]
            return xx + weight * (1.0 + gate) * rms_norm(y, post_g[l, j])

        h = pre(x, 0)
        x = post(x, swiglu(h, ffn_w_gate[l, 0], ffn_w_up[l, 0], ffn_w_down[l, 0]), 0, 0.5)

        h = pre(x, 1)
        proj = h @ w_in[l]
        q = proj[..., :ATTN_WIDTH]
        k = proj[..., ATTN_WIDTH:ATTN_WIDTH + KV_WIDTH]
        v = proj[..., ATTN_WIDTH + KV_WIDTH:ATTN_WIDTH + 2 * KV_WIDTH]
        u = proj[..., ATTN_WIDTH + 2 * KV_WIDTH:]
        a_out = rms_norm(windowed_gqa_attention(q, k, v, attn_sink[l], cos, sin, mask), branch_g[l, 0])
        f_out = rms_norm(fourier_mix(u, fourier_w[l]), branch_g[l, 1])
        y = jnp.concatenate([a_out, f_out], axis=-1) @ w_out[l]
        x = post(x, y, 1, 1.0)

        h = pre(x, 2)
        x = post(x, swiglu(h, ffn_w_gate[l, 1], ffn_w_up[l, 1], ffn_w_down[l, 1]), 2, 0.5)
    return x


def setup_inputs(seed: int = 0) -> dict:
    key = jax.random.key(seed)
    ks = jax.random.split(key, 18)
    f32 = jnp.float32

    def nrm(k, shape, scale):
        return jax.random.normal(k, shape, f32) * scale

    return {
        "x_prompt": nrm(ks[0], (BATCH, SEQ, D_MODEL), 1.0),
        "x_sample": nrm(ks[1], (DEC_BATCH, DEC_SEQ, D_MODEL), 1.0),
        "c_prompt": nrm(ks[2], (BATCH, D_MODEL), 1.0),
        "c_sample": nrm(ks[3], (DEC_BATCH, D_MODEL), 1.0),
        "w_mod": nrm(ks[4], (DEPTH, D_MODEL, N_SUBLAYERS * N_MOD * D_MODEL), MOD_SCALE * D_MODEL ** -0.5),
        "b_mod": nrm(ks[5], (DEPTH, N_SUBLAYERS * N_MOD * D_MODEL), 0.01),
        "pre_g": 1.0 + nrm(ks[6], (DEPTH, N_SUBLAYERS, D_MODEL), 0.05),
        "post_g": 1.0 + nrm(ks[7], (DEPTH, N_SUBLAYERS, D_MODEL), 0.05),
        "ffn_w_gate": nrm(ks[8], (DEPTH, 2, D_MODEL, D_FF), D_MODEL ** -0.5),
        "ffn_w_up": nrm(ks[9], (DEPTH, 2, D_MODEL, D_FF), D_MODEL ** -0.5),
        "ffn_w_down": nrm(ks[10], (DEPTH, 2, D_FF, D_MODEL), D_FF ** -0.5),
        "w_in": nrm(ks[11], (DEPTH, D_MODEL, IN_PROJ_WIDTH), D_MODEL ** -0.5),
        "attn_sink": nrm(ks[12], (DEPTH, N_Q_HEADS), 0.5),
        "fourier_w": nrm(ks[13], (DEPTH, N_FOURIER_GROUPS, FOURIER_GROUP_DIM, FOURIER_GROUP_DIM), FOURIER_GROUP_DIM ** -0.5),
        "branch_g": 1.0 + nrm(ks[14], (DEPTH, 2, ATTN_WIDTH), 0.05),
        "w_out": nrm(ks[15], (DEPTH, MIX_WIDTH, D_MODEL), MIX_WIDTH ** -0.5),
    }


def reference(x_prompt, x_sample, c_prompt, c_sample, w_mod, b_mod, pre_g, post_g,
              ffn_w_gate, ffn_w_up, ffn_w_down, w_in, attn_sink, fourier_w, branch_g, w_out):
    y_prompt = encoder_trunk(x_prompt, c_prompt, w_mod, b_mod, pre_g, post_g, ffn_w_gate, ffn_w_up,
                             ffn_w_down, w_in, attn_sink, fourier_w, branch_g, w_out)
    y_sample = encoder_trunk(x_sample, c_sample, w_mod, b_mod, pre_g, post_g, ffn_w_gate, ffn_w_up,
                             ffn_w_down, w_in, attn_sink, fourier_w, branch_g, w_out)
    return (y_prompt, y_sample)
```

```python
import functools

import numpy as np
import jax
import jax.numpy as jnp
from jax import lax
from jax.experimental import pallas as pl
from jax.experimental.pallas import tpu as pltpu

HEAD_DIM = 128
N_Q_HEADS = 8
N_KV_HEADS = 2
Q_PER_KV = N_Q_HEADS // N_KV_HEADS
ATTN_WIDTH = N_Q_HEADS * HEAD_DIM
KV_WIDTH = N_KV_HEADS * HEAD_DIM
N_FOURIER_GROUPS = 8
FOURIER_GROUP_DIM = 128
FOURIER_WIDTH = N_FOURIER_GROUPS * FOURIER_GROUP_DIM
WINDOW = 128
BLOCK = 128
ROPE_THETA = 10000.0
N_SUBLAYERS = 3
N_MOD = 3
RMS_EPS = 1e-6
NEG_INF = -1e30
MOD_ROWS = 8

V7X_VMEM_LIMIT_BYTES = 60 * 1024 * 1024

BF16 = jnp.bfloat16
F32 = jnp.float32


def _sigmoid(x):
    return 1.0 / (1.0 + jnp.exp(-x))


def _rms_scale(x):
    return lax.rsqrt(jnp.mean(x * x, axis=-1, keepdims=True) + RMS_EPS)


def _pre(x, gain, mod_ref):
    shift = mod_ref[0, 0:1, :]
    scale = mod_ref[0, 1:2, :]
    return (x * _rms_scale(x) * gain) * (1.0 + scale) + shift


def _post(x, y, gain, mod_ref, weight):
    gate = mod_ref[0, 2:3, :]
    return x + (weight * (1.0 + gate)) * (y * _rms_scale(y) * gain)


def _mod_kernel(c_ref, w_ref, b_ref, o_ref):
    c = c_ref[...]
    act = (c * _sigmoid(c)).astype(BF16)
    o_ref[0] = jnp.dot(act, w_ref[0].astype(BF16), preferred_element_type=F32) + b_ref[0]


def _modulation(c_pad, w_mod, b_mod):
    depth, d, width = w_mod.shape
    tn = 1024
    return pl.pallas_call(
        _mod_kernel,
        grid=(depth, width // tn),
        in_specs=[
            pl.BlockSpec((MOD_ROWS, d), lambda l, n: (0, 0)),
            pl.BlockSpec((1, d, tn), lambda l, n: (l, 0, n)),
            pl.BlockSpec((1, 1, tn), lambda l, n: (l, 0, n)),
        ],
        out_specs=pl.BlockSpec((1, MOD_ROWS, tn), lambda l, n: (l, 0, n)),
        out_shape=jax.ShapeDtypeStruct((depth, MOD_ROWS, width), F32),
        compiler_params=pltpu.CompilerParams(dimension_semantics=("parallel", "parallel")),
        name="modulation",
    )(c_pad, w_mod, b_mod.reshape(depth, 1, width))


def _ffn_kernel(x_ref, mod_ref, pre_g_ref, post_g_ref, wg_ref, wu_ref, wd_ref, o_ref, h_ref,
                *, n_chunks, weight):
    j = pl.program_id(1)

    @pl.when(j == 0)
    def _():
        h_ref[...] = _pre(x_ref[...], pre_g_ref[...], mod_ref).astype(BF16)
        o_ref[...] = jnp.zeros_like(o_ref)

    h = h_ref[...]
    g = jnp.dot(h, wg_ref[...], preferred_element_type=F32)
    u = jnp.dot(h, wu_ref[...], preferred_element_type=F32)
    a = ((g * _sigmoid(g)) * u).astype(BF16)
    o_ref[...] += jnp.dot(a, wd_ref[...], preferred_element_type=F32)

    @pl.when(j == n_chunks - 1)
    def _():
        o_ref[...] = _post(x_ref[...], o_ref[...], post_g_ref[...], mod_ref, weight)


def _ffn(x, mod, pre_g, post_g, wg, wu, wd, *, seq, weight, tm, tf):
    rows, d = x.shape
    d_ff = wg.shape[1]
    tiles_per_seq = seq // tm
    n_chunks = d_ff // tf
    return pl.pallas_call(
        functools.partial(_ffn_kernel, n_chunks=n_chunks, weight=weight),
        grid=(rows // tm, n_chunks),
        in_specs=[
            pl.BlockSpec((tm, d), lambda r, j: (r, 0)),
            pl.BlockSpec((1, N_MOD, d), lambda r, j: (r // tiles_per_seq, 0, 0)),
            pl.BlockSpec((1, d), lambda r, j: (0, 0)),
            pl.BlockSpec((1, d), lambda r, j: (0, 0)),
            pl.BlockSpec((d, tf), lambda r, j: (0, j)),
            pl.BlockSpec((d, tf), lambda r, j: (0, j)),
            pl.BlockSpec((tf, d), lambda r, j: (j, 0)),
        ],
        out_specs=pl.BlockSpec((tm, d), lambda r, j: (r, 0)),
        out_shape=jax.ShapeDtypeStruct((rows, d), F32),
        scratch_shapes=[pltpu.VMEM((tm, d), BF16)],
        compiler_params=pltpu.CompilerParams(
            dimension_semantics=("parallel", "arbitrary"),
            vmem_limit_bytes=V7X_VMEM_LIMIT_BYTES),
        name="ffn",
    )(x, mod, pre_g.reshape(1, d), post_g.reshape(1, d), wg, wu, wd)


def _inproj_kernel(x_ref, mod_ref, pre_g_ref, w_ref, cos_ref, sin_ref, wc_ref,
                   q_ref, k_ref, v_ref, pq_ref):
    h = _pre(x_ref[...], pre_g_ref[...], mod_ref).astype(BF16)
    proj = jnp.dot(h, w_ref[...], preferred_element_type=F32)
    cos_t = cos_ref[...]
    sin_t = sin_ref[...]

    def rope(t):
        return t * cos_t + pltpu.roll(t, HEAD_DIM // 2, 1) * sin_t

    q_scale = HEAD_DIM ** -0.5
    for hh in range(N_Q_HEADS):
        sl = slice(hh * HEAD_DIM, (hh + 1) * HEAD_DIM)
        q_ref[:, sl] = (rope(proj[:, sl]) * q_scale).astype(BF16)
    for hh in range(N_KV_HEADS):
        src = slice(ATTN_WIDTH + hh * HEAD_DIM, ATTN_WIDTH + (hh + 1) * HEAD_DIM)
        k_ref[:, hh * HEAD_DIM:(hh + 1) * HEAD_DIM] = rope(proj[:, src]).astype(BF16)
    v_ref[...] = proj[:, ATTN_WIDTH + KV_WIDTH:ATTN_WIDTH + 2 * KV_WIDTH].astype(BF16)
    u0 = ATTN_WIDTH + 2 * KV_WIDTH
    wc = wc_ref[...]
    for g in range(N_FOURIER_GROUPS):
        dst = slice(g * FOURIER_GROUP_DIM, (g + 1) * FOURIER_GROUP_DIM)
        ug = proj[:, u0 + g * FOURIER_GROUP_DIM:u0 + (g + 1) * FOURIER_GROUP_DIM].astype(BF16)
        pq = jnp.dot(ug, wc, preferred_element_type=F32)
        pq_ref[0, :, dst] = pq[:, :FOURIER_GROUP_DIM].astype(BF16)
        pq_ref[1, :, dst] = pq[:, FOURIER_GROUP_DIM:].astype(BF16)


def _inproj(x, mod, pre_g, w_in, cos_t, sin_t, wc, *, seq, tm):
    rows, d = x.shape
    tiles_per_seq = seq // tm
    width = w_in.shape[1]
    return pl.pallas_call(
        _inproj_kernel,
        grid=(rows // tm,),
        in_specs=[
            pl.BlockSpec((tm, d), lambda r: (r, 0)),
            pl.BlockSpec((1, N_MOD, d), lambda r: (r // tiles_per_seq, 0, 0)),
            pl.BlockSpec((1, d), lambda r: (0, 0)),
            pl.BlockSpec((d, width), lambda r: (0, 0)),
            pl.BlockSpec((tm, HEAD_DIM), lambda r: (r % tiles_per_seq, 0)),
            pl.BlockSpec((tm, HEAD_DIM), lambda r: (r % tiles_per_seq, 0)),
            pl.BlockSpec((FOURIER_GROUP_DIM, 2 * FOURIER_GROUP_DIM), lambda r: (0, 0)),
        ],
        out_specs=[
            pl.BlockSpec((tm, ATTN_WIDTH), lambda r: (r, 0)),
            pl.BlockSpec((tm, KV_WIDTH), lambda r: (r, 0)),
            pl.BlockSpec((tm, KV_WIDTH), lambda r: (r, 0)),
            pl.BlockSpec((2, tm, FOURIER_WIDTH), lambda r: (0, r, 0)),
        ],
        out_shape=[
            jax.ShapeDtypeStruct((rows, ATTN_WIDTH), BF16),
            jax.ShapeDtypeStruct((rows, KV_WIDTH), BF16),
            jax.ShapeDtypeStruct((rows, KV_WIDTH), BF16),
            jax.ShapeDtypeStruct((2, rows, FOURIER_WIDTH), BF16),
        ],
        compiler_params=pltpu.CompilerParams(
            dimension_semantics=("parallel",), vmem_limit_bytes=V7X_VMEM_LIMIT_BYTES),
        name="inproj",
    )(x, mod, pre_g.reshape(1, d), w_in, cos_t, sin_t, wc)


def _attn_kernel(sink_ref, q_ref, kp_ref, kc_ref, kn_ref, vp_ref, vc_ref, vn_ref, g_ref,
                 o_ref, kbuf, vbuf, *, q_blocks, blocks_per_seq):
    tq = q_blocks * BLOCK
    kbuf[0:BLOCK] = kp_ref[...]
    kbuf[BLOCK:BLOCK + tq] = kc_ref[...]
    kbuf[BLOCK + tq:2 * BLOCK + tq] = kn_ref[...]
    vbuf[0:BLOCK] = vp_ref[...]
    vbuf[BLOCK:BLOCK + tq] = vc_ref[...]
    vbuf[BLOCK + tq:2 * BLOCK + tq] = vn_ref[...]

    first_block = (pl.program_id(0) * q_blocks) % blocks_per_seq
    band = 3 * BLOCK
    rows = Q_PER_KV * BLOCK
    qi = lax.broadcasted_iota(jnp.int32, (rows, band), 0) % BLOCK
    kj = lax.broadcasted_iota(jnp.int32, (rows, band), 1)
    in_window = jnp.abs(qi + BLOCK - kj) <= WINDOW
    gain = g_ref[...]

    for b in range(q_blocks):
        n = first_block + b
        lo = jnp.where(n == 0, BLOCK, 0)
        hi = jnp.where(n == blocks_per_seq - 1, 2 * BLOCK, band)
        valid = in_window & (kj >= lo) & (kj < hi)
        r0 = b * BLOCK
        heads = []
        for hk in range(N_KV_HEADS):
            cols = slice(hk * HEAD_DIM, (hk + 1) * HEAD_DIM)
            qs = jnp.concatenate(
                [q_ref[r0:r0 + BLOCK, (hk * Q_PER_KV + g) * HEAD_DIM:(hk * Q_PER_KV + g + 1) * HEAD_DIM]
                 for g in range(Q_PER_KV)], axis=0)
            kb = kbuf[r0:r0 + band, cols]
            vb = vbuf[r0:r0 + band, cols]
            s = lax.dot_general(qs, kb, (((1,), (1,)), ((), ())), preferred_element_type=F32)
            s = jnp.where(valid, s, NEG_INF)
            sink = jnp.concatenate(
                [jnp.full((BLOCK, 1), sink_ref[hk * Q_PER_KV + g], F32) for g in range(Q_PER_KV)], axis=0)
            m = jnp.maximum(jnp.max(s, axis=-1, keepdims=True), sink)
            p = jnp.exp(s - m)
            denom = jnp.sum(p, axis=-1, keepdims=True) + jnp.exp(sink - m)
            probs = (p * (1.0 / denom)).astype(BF16)
            o = jnp.dot(probs, vb, preferred_element_type=F32)
            heads.extend(o[g * BLOCK:(g + 1) * BLOCK] for g in range(Q_PER_KV))
        sq = heads[0] * heads[0]
        for t in heads[1:]:
            sq = sq + t * t
        inv = lax.rsqrt(jnp.sum(sq, axis=-1, keepdims=True) * (1.0 / ATTN_WIDTH) + RMS_EPS)
        for hh, t in enumerate(heads):
            sl = slice(hh * HEAD_DIM, (hh + 1) * HEAD_DIM)
            o_ref[r0:r0 + BLOCK, sl] = (t * inv * gain[:, sl]).astype(BF16)


def _attention(q, k, v, sink, gain, *, seq, q_blocks):
    rows = q.shape[0]
    tq = q_blocks * BLOCK
    bps = seq // BLOCK

    def prev_map(r):
        g0 = r * q_blocks
        return (jnp.where(g0 % bps == 0, g0, g0 - 1), 0)

    def next_map(r):
        g1 = (r + 1) * q_blocks
        return (jnp.where(g1 % bps == 0, g1 - 1, g1), 0)

    edge = pl.BlockSpec((BLOCK, KV_WIDTH), prev_map)
    edge_n = pl.BlockSpec((BLOCK, KV_WIDTH), next_map)
    cur = pl.BlockSpec((tq, KV_WIDTH), lambda r: (r, 0))
    return pl.pallas_call(
        functools.partial(_attn_kernel, q_blocks=q_blocks, blocks_per_seq=bps),
        grid=(rows // tq,),
        in_specs=[
            pl.BlockSpec(memory_space=pltpu.SMEM),
            pl.BlockSpec((tq, ATTN_WIDTH), lambda r: (r, 0)),
            edge, cur, edge_n, edge, cur, edge_n,
            pl.BlockSpec((1, ATTN_WIDTH), lambda r: (0, 0)),
        ],
        out_specs=pl.BlockSpec((tq, ATTN_WIDTH), lambda r: (r, 0)),
        out_shape=jax.ShapeDtypeStruct((rows, ATTN_WIDTH), BF16),
        scratch_shapes=[pltpu.VMEM((tq + 2 * BLOCK, KV_WIDTH), BF16),
                        pltpu.VMEM((tq + 2 * BLOCK, KV_WIDTH), BF16)],
        compiler_params=pltpu.CompilerParams(dimension_semantics=("parallel",)),
        name="attention",
    )(sink, q, k, k, k, v, v, v, gain.reshape(1, ATTN_WIDTH))


def _fft_a_kernel(pq_ref, a_ref, tc_ref, ts_ref, o_ref, *, n_outer, cols):
    lw = pq_ref.shape[-1]
    x = pq_ref[...].reshape(2 * n_outer, lw)
    y = jnp.dot(a_ref[...], x, preferred_element_type=F32)
    reps = FOURIER_WIDTH // 128
    for t in range(cols):
        sl = slice(t * FOURIER_WIDTH, (t + 1) * FOURIER_WIDTH)
        yr = y[:n_outer, sl]
        yi = y[n_outer:, sl]
        tc = jnp.concatenate([tc_ref[t]] * reps, axis=1)
        ts = jnp.concatenate([ts_ref[t]] * reps, axis=1)
        o_ref[0, 0, :, sl] = (yr * tc + yi * ts).astype(BF16)
        o_ref[1, 0, :, sl] = (yi * tc - yr * ts).astype(BF16)


def _fft_a(pq, a_mat, tw_cos, tw_sin, *, batch, seq, cols):
    n_outer = seq // BLOCK
    lanes = BLOCK * FOURIER_WIDTH
    lw = cols * FOURIER_WIDTH
    pq4 = pq.reshape(2, batch, n_outer, lanes)
    out = pl.pallas_call(
        functools.partial(_fft_a_kernel, n_outer=n_outer, cols=cols),
        grid=(batch, BLOCK // cols),
        in_specs=[
            pl.BlockSpec((2, 1, n_outer, lw), lambda b, c: (0, b, 0, c)),
            pl.BlockSpec((2 * n_outer, 2 * n_outer), lambda b, c: (0, 0)),
            pl.BlockSpec((cols, n_outer, 128), lambda b, c: (c, 0, 0)),
            pl.BlockSpec((cols, n_outer, 128), lambda b, c: (c, 0, 0)),
        ],
        out_specs=pl.BlockSpec((2, 1, n_outer, lw), lambda b, c: (0, b, 0, c)),
        out_shape=jax.ShapeDtypeStruct((2, batch, n_outer, lanes), BF16),
        compiler_params=pltpu.CompilerParams(dimension_semantics=("parallel", "parallel")),
        name="fft_a",
    )(pq4, a_mat, tw_cos, tw_sin)
    return out.reshape(2, batch * seq, FOURIER_WIDTH)


def _fft_b_kernel(y_ref, c_ref, s_ref, wl_ref, g_ref, o_ref, *, k1_per_step):
    cm = c_ref[...]
    sm = s_ref[...]
    zs = []
    for r in range(k1_per_step):
        rs = slice(r * BLOCK, (r + 1) * BLOCK)
        z = (jnp.dot(cm, y_ref[0, rs, :], preferred_element_type=F32)
             + jnp.dot(sm, y_ref[1, rs, :], preferred_element_type=F32))
        zs.append(z.astype(BF16))
    z_all = jnp.concatenate(zs, axis=0) if k1_per_step > 1 else zs[0]
    outs = []
    for g in range(N_FOURIER_GROUPS):
        sl = slice(g * FOURIER_GROUP_DIM, (g + 1) * FOURIER_GROUP_DIM)
        outs.append(jnp.dot(z_all[:, sl], wl_ref[g], preferred_element_type=F32))
    sq = outs[0] * outs[0]
    for t in outs[1:]:
        sq = sq + t * t
    inv = lax.rsqrt(jnp.sum(sq, axis=-1, keepdims=True) * (1.0 / FOURIER_WIDTH) + RMS_EPS)
    gain = g_ref[...]
    for g, t in enumerate(outs):
        sl = slice(g * FOURIER_GROUP_DIM, (g + 1) * FOURIER_GROUP_DIM)
        normed = (t * inv * gain[:, sl]).astype(BF16)
        for r in range(k1_per_step):
            o_ref[0, :, r * FOURIER_WIDTH + g * FOURIER_GROUP_DIM:
                  r * FOURIER_WIDTH + (g + 1) * FOURIER_GROUP_DIM] = normed[r * BLOCK:(r + 1) * BLOCK]


def _fft_b(yy, c_mat, s_mat, w_lin, gain, *, batch, seq, k1_per_step):
    n_outer = seq // BLOCK
    steps = n_outer // k1_per_step
    out = pl.pallas_call(
        functools.partial(_fft_b_kernel, k1_per_step=k1_per_step),
        grid=(batch, steps),
        in_specs=[
            pl.BlockSpec((2, k1_per_step * BLOCK, FOURIER_WIDTH), lambda b, i: (0, b * steps + i, 0)),
            pl.BlockSpec((BLOCK, BLOCK), lambda b, i: (0, 0)),
            pl.BlockSpec((BLOCK, BLOCK), lambda b, i: (0, 0)),
            pl.BlockSpec((N_FOURIER_GROUPS, FOURIER_GROUP_DIM, FOURIER_GROUP_DIM), lambda b, i: (0, 0, 0)),
            pl.BlockSpec((1, FOURIER_WIDTH), lambda b, i: (0, 0)),
        ],
        out_specs=pl.BlockSpec((1, BLOCK, k1_per_step * FOURIER_WIDTH), lambda b, i: (b, 0, i)),
        out_shape=jax.ShapeDtypeStruct((batch, BLOCK, n_outer * FOURIER_WIDTH), BF16),
        compiler_params=pltpu.CompilerParams(dimension_semantics=("parallel", "parallel")),
        name="fft_b",
    )(yy, c_mat, s_mat, w_lin, gain.reshape(1, FOURIER_WIDTH))
    return out.reshape(batch * seq, FOURIER_WIDTH)


def _outproj_kernel(a_ref, f_ref, x_ref, mod_ref, post_g_ref, w_ref, o_ref):
    y = (jnp.dot(a_ref[...], w_ref[:ATTN_WIDTH, :], preferred_element_type=F32)
         + jnp.dot(f_ref[...], w_ref[ATTN_WIDTH:, :], preferred_element_type=F32))
    o_ref[...] = _post(x_ref[...], y, post_g_ref[...], mod_ref, 1.0)


def _outproj(a, f, x, mod, post_g, w_out, *, seq, tm):
    rows, d = x.shape
    tiles_per_seq = seq // tm
    return pl.pallas_call(
        _outproj_kernel,
        grid=(rows // tm,),
        in_specs=[
            pl.BlockSpec((tm, ATTN_WIDTH), lambda r: (r, 0)),
            pl.BlockSpec((tm, FOURIER_WIDTH), lambda r: (r, 0)),
            pl.BlockSpec((tm, d), lambda r: (r, 0)),
            pl.BlockSpec((1, N_MOD, d), lambda r: (r // tiles_per_seq, 0, 0)),
            pl.BlockSpec((1, d), lambda r: (0, 0)),
            pl.BlockSpec((ATTN_WIDTH + FOURIER_WIDTH, d), lambda r: (0, 0)),
        ],
        out_specs=pl.BlockSpec((tm, d), lambda r: (r, 0)),
        out_shape=jax.ShapeDtypeStruct((rows, d), F32),
        compiler_params=pltpu.CompilerParams(
            dimension_semantics=("parallel",), vmem_limit_bytes=V7X_VMEM_LIMIT_BYTES),
        name="outproj",
    )(a, f, x, mod, post_g.reshape(1, d), w_out)


def _rope_tables(seq):
    inv_freq = ROPE_THETA ** (-jnp.arange(0, HEAD_DIM, 2, dtype=F32) / HEAD_DIM)
    ang = jnp.arange(seq, dtype=F32)[:, None] * inv_freq[None, :]
    cos, sin = jnp.cos(ang), jnp.sin(ang)
    return jnp.concatenate([cos, cos], axis=-1), jnp.concatenate([-sin, sin], axis=-1)


def _dft_tables(seq):
    n_outer = seq // BLOCK

    def cs(n, scale):
        idx = np.arange(n)
        ang = 2.0 * np.pi * ((idx[:, None] * idx[None, :]) % n) / n
        return np.cos(ang) * scale, np.sin(ang) * scale

    cc, sc = cs(FOURIER_GROUP_DIM, FOURIER_GROUP_DIM ** -0.5)
    wc = np.concatenate([cc, sc], axis=1)
    co, so = cs(n_outer, n_outer ** -0.5)
    a_mat = np.block([[co, -so], [-so, -co]])
    c128, s128 = cs(BLOCK, BLOCK ** -0.5)
    n2 = np.arange(BLOCK)[:, None]
    k1 = np.arange(n_outer)[None, :]
    tw = 2.0 * np.pi * ((n2 * k1) % seq) / seq
    tw_cos = np.broadcast_to(np.cos(tw)[:, :, None], (BLOCK, n_outer, 128))
    tw_sin = np.broadcast_to(np.sin(tw)[:, :, None], (BLOCK, n_outer, 128))
    as_f32 = lambda a: jnp.asarray(np.ascontiguousarray(a), dtype=F32)
    return (as_f32(wc).astype(BF16), as_f32(a_mat).astype(BF16), as_f32(tw_cos), as_f32(tw_sin),
            as_f32(c128).astype(BF16), as_f32(s128).astype(BF16))


def _tile(seq, want):
    return min(seq, want)


def _trunk(x, c, w_mod, b_mod, pre_g, post_g, ffn_w_gate, ffn_w_up, ffn_w_down,
           w_in, attn_sink, fourier_w, branch_g, w_out):
    batch, seq, d = x.shape
    depth = w_mod.shape[0]
    assert seq % BLOCK == 0 and batch <= MOD_ROWS
    n_outer = seq // BLOCK

    c_pad = jnp.zeros((MOD_ROWS, d), F32).at[:batch].set(c)
    mod = _modulation(c_pad, w_mod, b_mod).reshape(depth, MOD_ROWS, N_SUBLAYERS, N_MOD, d)

    cos_t, sin_t = _rope_tables(seq)
    wc, a_mat, tw_cos, tw_sin, c128, s128 = _dft_tables(seq)

    wg = ffn_w_gate.astype(BF16)
    wu = ffn_w_up.astype(BF16)
    wd = ffn_w_down.astype(BF16)
    w_in_b = w_in.astype(BF16)
    w_out_b = w_out.astype(BF16)
    w_lin = fourier_w.astype(BF16)

    tm_ffn = _tile(seq, 512)
    tf = 512 if wg.shape[-1] % 512 == 0 else wg.shape[-1]
    tm_proj = _tile(seq, 512)
    q_blocks = min(4, n_outer)
    fft_cols = 8
    k1_per_step = min(4, n_outer)

    xs = x.reshape(batch * seq, d)
    for l in range(depth):
        xs = _ffn(xs, mod[l, :, 0], pre_g[l, 0], post_g[l, 0], wg[l, 0], wu[l, 0], wd[l, 0],
                  seq=seq, weight=0.5, tm=tm_ffn, tf=tf)
        q, k, v, pq = _inproj(xs, mod[l, :, 1], pre_g[l, 1], w_in_b[l], cos_t, sin_t, wc,
                              seq=seq, tm=tm_proj)
        a_out = _attention(q, k, v, attn_sink[l], branch_g[l, 0], seq=seq, q_blocks=q_blocks)
        yy = _fft_a(pq, a_mat, tw_cos, tw_sin, batch=batch, seq=seq, cols=fft_cols)
        f_out = _fft_b(yy, c128, s128, w_lin[l], branch_g[l, 1], batch=batch, seq=seq,
                       k1_per_step=k1_per_step)
        xs = _outproj(a_out, f_out, xs, mod[l, :, 1], post_g[l, 1], w_out_b[l], seq=seq, tm=tm_proj)
        xs = _ffn(xs, mod[l, :, 2], pre_g[l, 2], post_g[l, 2], wg[l, 1], wu[l, 1], wd[l, 1],
                  seq=seq, weight=0.5, tm=tm_ffn, tf=tf)
    return xs


def kernel(x_prompt, x_sample, c_prompt, c_sample, w_mod, b_mod, pre_g, post_g, ffn_w_gate, ffn_w_up,
           ffn_w_down, w_in, attn_sink, fourier_w, branch_g, w_out):
    n_prompt, seq, d = x_prompt.shape
    assert x_sample.shape[1:] == (seq, d)
    x = jnp.concatenate([x_prompt, x_sample], axis=0)
    c = jnp.concatenate([c_prompt, c_sample], axis=0)
    y = _trunk(x, c, w_mod, b_mod, pre_g, post_g, ffn_w_gate, ffn_w_up, ffn_w_down,
               w_in, attn_sink, fourier_w, branch_g, w_out)
    y = y.reshape(x.shape)
    return y[:n_prompt], y[n_prompt:]
```

```python
import functools

import numpy as np
import jax
import jax.numpy as jnp
from jax import lax
from jax.experimental import pallas as pl
from jax.experimental.pallas import tpu as pltpu

HEAD_DIM = 128
N_Q_HEADS = 8
N_KV_HEADS = 2
Q_PER_KV = N_Q_HEADS // N_KV_HEADS
ATTN_WIDTH = N_Q_HEADS * HEAD_DIM
KV_WIDTH = N_KV_HEADS * HEAD_DIM
N_FOURIER_GROUPS = 8
FOURIER_GROUP_DIM = 128
FOURIER_WIDTH = N_FOURIER_GROUPS * FOURIER_GROUP_DIM
WINDOW = 128
BLOCK = 128
ROPE_THETA = 10000.0
N_SUBLAYERS = 3
N_MOD = 3
RMS_EPS = 1e-6
NEG_INF = -1e30
MOD_ROWS = 8

V7X_VMEM_LIMIT_BYTES = 60 * 1024 * 1024

BF16 = jnp.bfloat16
F32 = jnp.float32


def _sigmoid(x):
    return 1.0 / (1.0 + jnp.exp(-x))


LANES = 128
ROW_CHUNK = 128


def _row_rms_scale(ref, r0, rows):
    d = ref.shape[-1]
    acc = None
    for c0 in range(0, d, LANES):
        t = ref[r0:r0 + rows, c0:c0 + LANES]
        acc = t * t if acc is None else acc + t * t
    return lax.rsqrt(jnp.sum(acc, axis=-1, keepdims=True) * (1.0 / d) + RMS_EPS)


def _pre_into(h_ref, x_ref, gain_ref, mod_ref):
    rows, d = x_ref.shape
    rc = min(ROW_CHUNK, rows)
    for r0 in range(0, rows, rc):
        inv = _row_rms_scale(x_ref, r0, rc)
        for c0 in range(0, d, LANES):
            sl = slice(c0, c0 + LANES)
            t = (x_ref[r0:r0 + rc, sl] * inv * gain_ref[:, sl]) * (1.0 + mod_ref[0, 1:2, sl]) + mod_ref[0, 0:1, sl]
            h_ref[r0:r0 + rc, sl] = t.astype(h_ref.dtype)


def _post_into(o_ref, x_ref, gain_ref, mod_ref, weight):
    rows, d = x_ref.shape
    rc = min(ROW_CHUNK, rows)
    for r0 in range(0, rows, rc):
        inv = _row_rms_scale(o_ref, r0, rc)
        for c0 in range(0, d, LANES):
            sl = slice(c0, c0 + LANES)
            y = o_ref[r0:r0 + rc, sl]
            o_ref[r0:r0 + rc, sl] = (x_ref[r0:r0 + rc, sl]
                                     + (weight * (1.0 + mod_ref[0, 2:3, sl])) * (y * inv * gain_ref[:, sl]))


def _mod_kernel(c_ref, w_ref, b_ref, o_ref):
    c = c_ref[...]
    act = (c * _sigmoid(c)).astype(BF16)
    o_ref[0] = jnp.dot(act, w_ref[0].astype(BF16), preferred_element_type=F32) + b_ref[0]


def _modulation(c_pad, w_mod, b_mod):
    depth, d, width = w_mod.shape
    tn = 1024
    return pl.pallas_call(
        _mod_kernel,
        grid=(depth, width // tn),
        in_specs=[
            pl.BlockSpec((MOD_ROWS, d), lambda l, n: (0, 0)),
            pl.BlockSpec((1, d, tn), lambda l, n: (l, 0, n)),
            pl.BlockSpec((1, 1, tn), lambda l, n: (l, 0, n)),
        ],
        out_specs=pl.BlockSpec((1, MOD_ROWS, tn), lambda l, n: (l, 0, n)),
        out_shape=jax.ShapeDtypeStruct((depth, MOD_ROWS, width), F32),
        compiler_params=pltpu.CompilerParams(dimension_semantics=("parallel", "parallel")),
        name="modulation",
    )(c_pad, w_mod, b_mod.reshape(depth, 1, width))


def _ffn_kernel(x_ref, mod_ref, pre_g_ref, post_g_ref, wg_ref, wu_ref, wd_ref, o_ref, h_ref,
                *, n_chunks, weight):
    j = pl.program_id(1)

    @pl.when(j == 0)
    def _():
        _pre_into(h_ref, x_ref, pre_g_ref, mod_ref)
        o_ref[...] = jnp.zeros_like(o_ref)

    h = h_ref[...]
    g = jnp.dot(h, wg_ref[...], preferred_element_type=F32)
    u = jnp.dot(h, wu_ref[...], preferred_element_type=F32)
    a = ((g * _sigmoid(g)) * u).astype(BF16)
    o_ref[...] += jnp.dot(a, wd_ref[...], preferred_element_type=F32)

    @pl.when(j == n_chunks - 1)
    def _():
        _post_into(o_ref, x_ref, post_g_ref, mod_ref, weight)


def _ffn_carry_kernel(x_ref, mod_ref, pre_g_ref, post_g_ref, wg_ref, wu_ref, wd_ref, carry_ref, o_ref, h_ref,
                      **kw):
    del carry_ref
    _ffn_kernel(x_ref, mod_ref, pre_g_ref, post_g_ref, wg_ref, wu_ref, wd_ref, o_ref, h_ref, **kw)


def _ffn(x, mod, pre_g, post_g, wg, wu, wd, *, seq, weight, tm, tf,
         tiles=None, in_tile0=0, out_tile0=0, seq_tile0=0, out_rows=None, carry=None):
    d = x.shape[1]
    d_ff = wg.shape[1]
    tiles = x.shape[0] // tm if tiles is None else tiles
    out_rows = x.shape[0] if out_rows is None else out_rows
    tiles_per_seq = seq // tm
    n_chunks = d_ff // tf
    in_specs = [
        pl.BlockSpec((tm, d), lambda r, j: (r + in_tile0, 0)),
        pl.BlockSpec((1, N_MOD, d), lambda r, j: ((r + seq_tile0) // tiles_per_seq, 0, 0)),
        pl.BlockSpec((1, d), lambda r, j: (0, 0)),
        pl.BlockSpec((1, d), lambda r, j: (0, 0)),
        pl.BlockSpec((d, tf), lambda r, j: (0, j)),
        pl.BlockSpec((d, tf), lambda r, j: (0, j)),
        pl.BlockSpec((tf, d), lambda r, j: (j, 0)),
    ]
    args = [x, mod, pre_g.reshape(1, d), post_g.reshape(1, d), wg, wu, wd]
    body, aliases = _ffn_kernel, {}
    if carry is not None:
        in_specs.append(pl.BlockSpec(memory_space=pl.ANY))
        args.append(carry)
        body, aliases = _ffn_carry_kernel, {len(args) - 1: 0}
    return pl.pallas_call(
        functools.partial(body, n_chunks=n_chunks, weight=weight),
        grid=(tiles, n_chunks),
        in_specs=in_specs,
        out_specs=pl.BlockSpec((tm, d), lambda r, j: (r + out_tile0, 0)),
        out_shape=jax.ShapeDtypeStruct((out_rows, d), F32),
        scratch_shapes=[pltpu.VMEM((tm, d), BF16)],
        input_output_aliases=aliases,
        compiler_params=pltpu.CompilerParams(
            dimension_semantics=("parallel", "arbitrary"),
            vmem_limit_bytes=V7X_VMEM_LIMIT_BYTES),
        name="ffn",
    )(*args)


def _inproj_kernel(x_ref, mod_ref, pre_g_ref, w_ref, cos_ref, sin_ref, wc_ref,
                   q_ref, k_ref, v_ref, pq_ref, h_ref):
    _pre_into(h_ref, x_ref, pre_g_ref, mod_ref)
    proj = jnp.dot(h_ref[...], w_ref[...], preferred_element_type=F32)
    cos_t = cos_ref[...]
    sin_t = sin_ref[...]

    def rope(t):
        return t * cos_t + pltpu.roll(t, HEAD_DIM // 2, 1) * sin_t

    q_scale = HEAD_DIM ** -0.5
    for hh in range(N_Q_HEADS):
        sl = slice(hh * HEAD_DIM, (hh + 1) * HEAD_DIM)
        q_ref[:, sl] = (rope(proj[:, sl]) * q_scale).astype(BF16)
    for hh in range(N_KV_HEADS):
        src = slice(ATTN_WIDTH + hh * HEAD_DIM, ATTN_WIDTH + (hh + 1) * HEAD_DIM)
        k_ref[:, hh * HEAD_DIM:(hh + 1) * HEAD_DIM] = rope(proj[:, src]).astype(BF16)
    v_ref[...] = proj[:, ATTN_WIDTH + KV_WIDTH:ATTN_WIDTH + 2 * KV_WIDTH].astype(BF16)
    u0 = ATTN_WIDTH + 2 * KV_WIDTH
    wc = wc_ref[...]
    for g in range(N_FOURIER_GROUPS):
        dst = slice(g * FOURIER_GROUP_DIM, (g + 1) * FOURIER_GROUP_DIM)
        ug = proj[:, u0 + g * FOURIER_GROUP_DIM:u0 + (g + 1) * FOURIER_GROUP_DIM].astype(BF16)
        pq = jnp.dot(ug, wc, preferred_element_type=F32)
        pq_ref[0, :, dst] = pq[:, :FOURIER_GROUP_DIM].astype(BF16)
        pq_ref[1, :, dst] = pq[:, FOURIER_GROUP_DIM:].astype(BF16)


def _inproj(x, mod, pre_g, w_in, cos_t, sin_t, wc, *, seq, tm):
    rows, d = x.shape
    tiles_per_seq = seq // tm
    width = w_in.shape[1]
    return pl.pallas_call(
        _inproj_kernel,
        grid=(rows // tm,),
        in_specs=[
            pl.BlockSpec((tm, d), lambda r: (r, 0)),
            pl.BlockSpec((1, N_MOD, d), lambda r: (r // tiles_per_seq, 0, 0)),
            pl.BlockSpec((1, d), lambda r: (0, 0)),
            pl.BlockSpec((d, width), lambda r: (0, 0)),
            pl.BlockSpec((tm, HEAD_DIM), lambda r: (r % tiles_per_seq, 0)),
            pl.BlockSpec((tm, HEAD_DIM), lambda r: (r % tiles_per_seq, 0)),
            pl.BlockSpec((FOURIER_GROUP_DIM, 2 * FOURIER_GROUP_DIM), lambda r: (0, 0)),
        ],
        out_specs=[
            pl.BlockSpec((tm, ATTN_WIDTH), lambda r: (r, 0)),
            pl.BlockSpec((tm, KV_WIDTH), lambda r: (r, 0)),
            pl.BlockSpec((tm, KV_WIDTH), lambda r: (r, 0)),
            pl.BlockSpec((2, tm, FOURIER_WIDTH), lambda r: (0, r, 0)),
        ],
        out_shape=[
            jax.ShapeDtypeStruct((rows, ATTN_WIDTH), BF16),
            jax.ShapeDtypeStruct((rows, KV_WIDTH), BF16),
            jax.ShapeDtypeStruct((rows, KV_WIDTH), BF16),
            jax.ShapeDtypeStruct((2, rows, FOURIER_WIDTH), BF16),
        ],
        scratch_shapes=[pltpu.VMEM((tm, d), BF16)],
        compiler_params=pltpu.CompilerParams(
            dimension_semantics=("parallel",), vmem_limit_bytes=V7X_VMEM_LIMIT_BYTES),
        name="inproj",
    )(x, mod, pre_g.reshape(1, d), w_in, cos_t, sin_t, wc)


def _attn_kernel(sink_ref, q_ref, kp_ref, kc_ref, kn_ref, vp_ref, vc_ref, vn_ref, g_ref,
                 o_ref, kbuf, vbuf, *, q_blocks, blocks_per_seq):
    tq = q_blocks * BLOCK
    kbuf[0:BLOCK] = kp_ref[...]
    kbuf[BLOCK:BLOCK + tq] = kc_ref[...]
    kbuf[BLOCK + tq:2 * BLOCK + tq] = kn_ref[...]
    vbuf[0:BLOCK] = vp_ref[...]
    vbuf[BLOCK:BLOCK + tq] = vc_ref[...]
    vbuf[BLOCK + tq:2 * BLOCK + tq] = vn_ref[...]

    first_block = (pl.program_id(0) * q_blocks) % blocks_per_seq
    band = 3 * BLOCK
    rows = Q_PER_KV * BLOCK
    qi = lax.broadcasted_iota(jnp.int32, (rows, band), 0) % BLOCK
    kj = lax.broadcasted_iota(jnp.int32, (rows, band), 1)
    in_window = jnp.abs(qi + BLOCK - kj) <= WINDOW
    gain = g_ref[...]

    for b in range(q_blocks):
        n = first_block + b
        lo = jnp.where(n == 0, BLOCK, 0)
        hi = jnp.where(n == blocks_per_seq - 1, 2 * BLOCK, band)
        valid = in_window & (kj >= lo) & (kj < hi)
        r0 = b * BLOCK
        heads = []
        for hk in range(N_KV_HEADS):
            cols = slice(hk * HEAD_DIM, (hk + 1) * HEAD_DIM)
            qs = jnp.concatenate(
                [q_ref[r0:r0 + BLOCK, (hk * Q_PER_KV + g) * HEAD_DIM:(hk * Q_PER_KV + g + 1) * HEAD_DIM]
                 for g in range(Q_PER_KV)], axis=0)
            kb = kbuf[r0:r0 + band, cols]
            vb = vbuf[r0:r0 + band, cols]
            s = lax.dot_general(qs, kb, (((1,), (1,)), ((), ())), preferred_element_type=F32)
            s = jnp.where(valid, s, NEG_INF)
            sink = jnp.concatenate(
                [jnp.full((BLOCK, 1), sink_ref[hk * Q_PER_KV + g], F32) for g in range(Q_PER_KV)], axis=0)
            m = jnp.maximum(jnp.max(s, axis=-1, keepdims=True), sink)
            p = jnp.exp(s - m)
            denom = jnp.sum(p, axis=-1, keepdims=True) + jnp.exp(sink - m)
            probs = (p * (1.0 / denom)).astype(BF16)
            o = jnp.dot(probs, vb, preferred_element_type=F32)
            heads.extend(o[g * BLOCK:(g + 1) * BLOCK] for g in range(Q_PER_KV))
        sq = heads[0] * heads[0]
        for t in heads[1:]:
            sq = sq + t * t
        inv = lax.rsqrt(jnp.sum(sq, axis=-1, keepdims=True) * (1.0 / ATTN_WIDTH) + RMS_EPS)
        for hh, t in enumerate(heads):
            sl = slice(hh * HEAD_DIM, (hh + 1) * HEAD_DIM)
            o_ref[r0:r0 + BLOCK, sl] = (t * inv * gain[:, sl]).astype(BF16)


def _attention(q, k, v, sink, gain, *, seq, q_blocks):
    rows = q.shape[0]
    tq = q_blocks * BLOCK
    bps = seq // BLOCK

    def prev_map(r):
        g0 = r * q_blocks
        return (jnp.where(g0 % bps == 0, g0, g0 - 1), 0)

    def next_map(r):
        g1 = (r + 1) * q_blocks
        return (jnp.where(g1 % bps == 0, g1 - 1, g1), 0)

    edge = pl.BlockSpec((BLOCK, KV_WIDTH), prev_map)
    edge_n = pl.BlockSpec((BLOCK, KV_WIDTH), next_map)
    cur = pl.BlockSpec((tq, KV_WIDTH), lambda r: (r, 0))
    return pl.pallas_call(
        functools.partial(_attn_kernel, q_blocks=q_blocks, blocks_per_seq=bps),
        grid=(rows // tq,),
        in_specs=[
            pl.BlockSpec(memory_space=pltpu.SMEM),
            pl.BlockSpec((tq, ATTN_WIDTH), lambda r: (r, 0)),
            edge, cur, edge_n, edge, cur, edge_n,
            pl.BlockSpec((1, ATTN_WIDTH), lambda r: (0, 0)),
        ],
        out_specs=pl.BlockSpec((tq, ATTN_WIDTH), lambda r: (r, 0)),
        out_shape=jax.ShapeDtypeStruct((rows, ATTN_WIDTH), BF16),
        scratch_shapes=[pltpu.VMEM((tq + 2 * BLOCK, KV_WIDTH), BF16),
                        pltpu.VMEM((tq + 2 * BLOCK, KV_WIDTH), BF16)],
        compiler_params=pltpu.CompilerParams(dimension_semantics=("parallel",)),
        name="attention",
    )(sink, q, k, k, k, v, v, v, gain.reshape(1, ATTN_WIDTH))


def _fft_a_kernel(pq_ref, a_ref, tc_ref, ts_ref, o_ref, *, n_outer, cols):
    lw = pq_ref.shape[-1]
    x = pq_ref[...].reshape(2 * n_outer, lw)
    y = jnp.dot(a_ref[...], x, preferred_element_type=F32)
    reps = FOURIER_WIDTH // 128
    for t in range(cols):
        sl = slice(t * FOURIER_WIDTH, (t + 1) * FOURIER_WIDTH)
        yr = y[:n_outer, sl]
        yi = y[n_outer:, sl]
        tc = jnp.concatenate([tc_ref[t]] * reps, axis=1)
        ts = jnp.concatenate([ts_ref[t]] * reps, axis=1)
        o_ref[0, 0, :, sl] = (yr * tc + yi * ts).astype(BF16)
        o_ref[1, 0, :, sl] = (yi * tc - yr * ts).astype(BF16)


def _fft_a(pq, a_mat, tw_cos, tw_sin, *, batch, seq, cols):
    n_outer = seq // BLOCK
    lanes = BLOCK * FOURIER_WIDTH
    lw = cols * FOURIER_WIDTH
    pq4 = pq.reshape(2, batch, n_outer, lanes)
    out = pl.pallas_call(
        functools.partial(_fft_a_kernel, n_outer=n_outer, cols=cols),
        grid=(batch, BLOCK // cols),
        in_specs=[
            pl.BlockSpec((2, 1, n_outer, lw), lambda b, c: (0, b, 0, c)),
            pl.BlockSpec((2 * n_outer, 2 * n_outer), lambda b, c: (0, 0)),
            pl.BlockSpec((cols, n_outer, 128), lambda b, c: (c, 0, 0)),
            pl.BlockSpec((cols, n_outer, 128), lambda b, c: (c, 0, 0)),
        ],
        out_specs=pl.BlockSpec((2, 1, n_outer, lw), lambda b, c: (0, b, 0, c)),
        out_shape=jax.ShapeDtypeStruct((2, batch, n_outer, lanes), BF16),
        compiler_params=pltpu.CompilerParams(dimension_semantics=("parallel", "parallel")),
        name="fft_a",
    )(pq4, a_mat, tw_cos, tw_sin)
    return out.reshape(2, batch * seq, FOURIER_WIDTH)


def _fft_b_kernel(y_ref, c_ref, s_ref, wl_ref, g_ref, o_ref, *, k1_per_step):
    cm = c_ref[...]
    sm = s_ref[...]
    zs = []
    for r in range(k1_per_step):
        rs = slice(r * BLOCK, (r + 1) * BLOCK)
        z = (jnp.dot(cm, y_ref[0, rs, :], preferred_element_type=F32)
             + jnp.dot(sm, y_ref[1, rs, :], preferred_element_type=F32))
        zs.append(z.astype(BF16))
    z_all = jnp.concatenate(zs, axis=0) if k1_per_step > 1 else zs[0]
    outs = []
    for g in range(N_FOURIER_GROUPS):
        sl = slice(g * FOURIER_GROUP_DIM, (g + 1) * FOURIER_GROUP_DIM)
        outs.append(jnp.dot(z_all[:, sl], wl_ref[g], preferred_element_type=F32))
    sq = outs[0] * outs[0]
    for t in outs[1:]:
        sq = sq + t * t
    inv = lax.rsqrt(jnp.sum(sq, axis=-1, keepdims=True) * (1.0 / FOURIER_WIDTH) + RMS_EPS)
    gain = g_ref[...]
    for g, t in enumerate(outs):
        sl = slice(g * FOURIER_GROUP_DIM, (g + 1) * FOURIER_GROUP_DIM)
        normed = (t * inv * gain[:, sl]).astype(BF16)
        for r in range(k1_per_step):
            o_ref[0, :, r * FOURIER_WIDTH + g * FOURIER_GROUP_DIM:
                  r * FOURIER_WIDTH + (g + 1) * FOURIER_GROUP_DIM] = normed[r * BLOCK:(r + 1) * BLOCK]


def _fft_b(yy, c_mat, s_mat, w_lin, gain, *, batch, seq, k1_per_step):
    n_outer = seq // BLOCK
    steps = n_outer // k1_per_step
    out = pl.pallas_call(
        functools.partial(_fft_b_kernel, k1_per_step=k1_per_step),
        grid=(batch, steps),
        in_specs=[
            pl.BlockSpec((2, k1_per_step * BLOCK, FOURIER_WIDTH), lambda b, i: (0, b * steps + i, 0)),
            pl.BlockSpec((BLOCK, BLOCK), lambda b, i: (0, 0)),
            pl.BlockSpec((BLOCK, BLOCK), lambda b, i: (0, 0)),
            pl.BlockSpec((N_FOURIER_GROUPS, FOURIER_GROUP_DIM, FOURIER_GROUP_DIM), lambda b, i: (0, 0, 0)),
            pl.BlockSpec((1, FOURIER_WIDTH), lambda b, i: (0, 0)),
        ],
        out_specs=pl.BlockSpec((1, BLOCK, k1_per_step * FOURIER_WIDTH), lambda b, i: (b, 0, i)),
        out_shape=jax.ShapeDtypeStruct((batch, BLOCK, n_outer * FOURIER_WIDTH), BF16),
        compiler_params=pltpu.CompilerParams(dimension_semantics=("parallel", "parallel")),
        name="fft_b",
    )(yy, c_mat, s_mat, w_lin, gain.reshape(1, FOURIER_WIDTH))
    return out.reshape(batch * seq, FOURIER_WIDTH)


def _outproj_kernel(a_ref, f_ref, x_ref, mod_ref, post_g_ref, w_ref, o_ref):
    o_ref[...] = (jnp.dot(a_ref[...], w_ref[:ATTN_WIDTH, :], preferred_element_type=F32)
                  + jnp.dot(f_ref[...], w_ref[ATTN_WIDTH:, :], preferred_element_type=F32))
    _post_into(o_ref, x_ref, post_g_ref, mod_ref, 1.0)


def _outproj(a, f, x, mod, post_g, w_out, *, seq, tm):
    rows, d = x.shape
    tiles_per_seq = seq // tm
    return pl.pallas_call(
        _outproj_kernel,
        grid=(rows // tm,),
        in_specs=[
            pl.BlockSpec((tm, ATTN_WIDTH), lambda r: (r, 0)),
            pl.BlockSpec((tm, FOURIER_WIDTH), lambda r: (r, 0)),
            pl.BlockSpec((tm, d), lambda r: (r, 0)),
            pl.BlockSpec((1, N_MOD, d), lambda r: (r // tiles_per_seq, 0, 0)),
            pl.BlockSpec((1, d), lambda r: (0, 0)),
            pl.BlockSpec((ATTN_WIDTH + FOURIER_WIDTH, d), lambda r: (0, 0)),
        ],
        out_specs=pl.BlockSpec((tm, d), lambda r: (r, 0)),
        out_shape=jax.ShapeDtypeStruct((rows, d), F32),
        compiler_params=pltpu.CompilerParams(
            dimension_semantics=("parallel",), vmem_limit_bytes=V7X_VMEM_LIMIT_BYTES),
        name="outproj",
    )(a, f, x, mod, post_g.reshape(1, d), w_out)


def _rope_tables(seq):
    inv_freq = ROPE_THETA ** (-jnp.arange(0, HEAD_DIM, 2, dtype=F32) / HEAD_DIM)
    ang = jnp.arange(seq, dtype=F32)[:, None] * inv_freq[None, :]
    cos, sin = jnp.cos(ang), jnp.sin(ang)
    return jnp.concatenate([cos, cos], axis=-1), jnp.concatenate([-sin, sin], axis=-1)


def _dft_tables(seq):
    n_outer = seq // BLOCK

    def cs(n, scale):
        idx = np.arange(n)
        ang = 2.0 * np.pi * ((idx[:, None] * idx[None, :]) % n) / n
        return np.cos(ang) * scale, np.sin(ang) * scale

    cc, sc = cs(FOURIER_GROUP_DIM, FOURIER_GROUP_DIM ** -0.5)
    wc = np.concatenate([cc, sc], axis=1)
    co, so = cs(n_outer, n_outer ** -0.5)
    a_mat = np.block([[co, -so], [-so, -co]])
    c128, s128 = cs(BLOCK, BLOCK ** -0.5)
    n2 = np.arange(BLOCK)[:, None]
    k1 = np.arange(n_outer)[None, :]
    tw = 2.0 * np.pi * ((n2 * k1) % seq) / seq
    tw_cos = np.broadcast_to(np.cos(tw)[:, :, None], (BLOCK, n_outer, 128))
    tw_sin = np.broadcast_to(np.sin(tw)[:, :, None], (BLOCK, n_outer, 128))
    as_f32 = lambda a: jnp.asarray(np.ascontiguousarray(a), dtype=F32)
    return (as_f32(wc).astype(BF16), as_f32(a_mat).astype(BF16), as_f32(tw_cos), as_f32(tw_sin),
            as_f32(c128).astype(BF16), as_f32(s128).astype(BF16))


def _tile(seq, want):
    return min(seq, want)


def _trunk(x_groups, c_groups, w_mod, b_mod, pre_g, post_g, ffn_w_gate, ffn_w_up, ffn_w_down,
           w_in, attn_sink, fourier_w, branch_g, w_out):
    seq, d = x_groups[0].shape[1:]
    sizes = [x.shape[0] for x in x_groups]
    batch = sum(sizes)
    depth = w_mod.shape[0]
    assert seq % BLOCK == 0 and batch <= MOD_ROWS and all(x.shape[1:] == (seq, d) for x in x_groups)
    n_outer = seq // BLOCK

    c_pad = jnp.zeros((MOD_ROWS, d), F32).at[:batch].set(jnp.concatenate(c_groups, axis=0))
    mod = _modulation(c_pad, w_mod, b_mod).reshape(depth, MOD_ROWS, N_SUBLAYERS, N_MOD, d)

    cos_t, sin_t = _rope_tables(seq)
    wc, a_mat, tw_cos, tw_sin, c128, s128 = _dft_tables(seq)

    wg = ffn_w_gate.astype(BF16)
    wu = ffn_w_up.astype(BF16)
    wd = ffn_w_down.astype(BF16)
    w_in_b = w_in.astype(BF16)
    w_out_b = w_out.astype(BF16)
    w_lin = fourier_w.astype(BF16)

    tm_ffn = _tile(seq, 1024)
    tf = 512 if wg.shape[-1] % 512 == 0 else wg.shape[-1]
    tm_proj = _tile(seq, 512)
    q_blocks = min(4, n_outer)
    fft_cols = 8
    k1_per_step = min(4, n_outer)
    tiles_per_seq = seq // tm_ffn
    group_tile0 = [sum(sizes[:i]) * tiles_per_seq for i in range(len(sizes))]

    def ffn(x, l, sub, which, **kw):
        return _ffn(x, mod[l, :, sub], pre_g[l, sub], post_g[l, sub], wg[l, which], wu[l, which], wd[l, which],
                    seq=seq, weight=0.5, tm=tm_ffn, tf=tf, **kw)

    xs = None
    for l in range(depth):
        if l == 0:
            for gi in reversed(range(len(sizes))):
                xs = ffn(x_groups[gi].reshape(sizes[gi] * seq, d), l, 0, 0,
                         tiles=sizes[gi] * tiles_per_seq, out_tile0=group_tile0[gi], seq_tile0=group_tile0[gi],
                         out_rows=batch * seq, carry=xs)
        else:
            xs = ffn(xs, l, 0, 0)
        q, k, v, pq = _inproj(xs, mod[l, :, 1], pre_g[l, 1], w_in_b[l], cos_t, sin_t, wc,
                              seq=seq, tm=tm_proj)
        a_out = _attention(q, k, v, attn_sink[l], branch_g[l, 0], seq=seq, q_blocks=q_blocks)
        yy = _fft_a(pq, a_mat, tw_cos, tw_sin, batch=batch, seq=seq, cols=fft_cols)
        f_out = _fft_b(yy, c128, s128, w_lin[l], branch_g[l, 1], batch=batch, seq=seq,
                       k1_per_step=k1_per_step)
        xs = _outproj(a_out, f_out, xs, mod[l, :, 1], post_g[l, 1], w_out_b[l], seq=seq, tm=tm_proj)
        if l < depth - 1:
            xs = ffn(xs, l, 2, 1)
    outs = []
    for gi, n in enumerate(sizes):
        y = ffn(xs, depth - 1, 2, 1, tiles=n * tiles_per_seq, in_tile0=group_tile0[gi],
                seq_tile0=group_tile0[gi], out_rows=n * seq)
        outs.append(y.reshape(n, seq, d))
    return outs


def kernel(x_prompt, x_sample, c_prompt, c_sample, w_mod, b_mod, pre_g, post_g, ffn_w_gate, ffn_w_up,
           ffn_w_down, w_in, attn_sink, fourier_w, branch_g, w_out):
    y_prompt, y_sample = _trunk([x_prompt, x_sample], [c_prompt, c_sample], w_mod, b_mod, pre_g, post_g,
                                ffn_w_gate, ffn_w_up, ffn_w_down, w_in, attn_sink, fourier_w, branch_g, w_out)
    return y_prompt, y_sample
```

```python
import functools

import numpy as np
import jax
import jax.numpy as jnp
from jax import lax
from jax.experimental import pallas as pl
from jax.experimental.pallas import tpu as pltpu

HEAD_DIM = 128
N_Q_HEADS = 8
N_KV_HEADS = 2
Q_PER_KV = N_Q_HEADS // N_KV_HEADS
ATTN_WIDTH = N_Q_HEADS * HEAD_DIM
KV_WIDTH = N_KV_HEADS * HEAD_DIM
N_FOURIER_GROUPS = 8
FOURIER_GROUP_DIM = 128
FOURIER_WIDTH = N_FOURIER_GROUPS * FOURIER_GROUP_DIM
WINDOW = 128
BLOCK = 128
ROPE_THETA = 10000.0
N_SUBLAYERS = 3
N_MOD = 3
RMS_EPS = 1e-6
NEG_INF = -1e30
LOG2_E = 1.4426950408889634
MOD_ROWS = 8

V7X_VMEM_LIMIT_BYTES = 60 * 1024 * 1024

BF16 = jnp.bfloat16
F32 = jnp.float32


def _sigmoid(x):
    return 1.0 / (1.0 + jnp.exp(-x))


LANES = 128
ROW_CHUNK = 128


def _row_rms_scale(ref, r0, rows):
    d = ref.shape[-1]
    acc = None
    for c0 in range(0, d, LANES):
        t = ref[r0:r0 + rows, c0:c0 + LANES]
        acc = t * t if acc is None else acc + t * t
    return lax.rsqrt(jnp.sum(acc, axis=-1, keepdims=True) * (1.0 / d) + RMS_EPS)


def _pre_into(h_ref, x_ref, gain_ref, mod_ref):
    rows, d = x_ref.shape
    rc = min(ROW_CHUNK, rows)
    for r0 in range(0, rows, rc):
        inv = _row_rms_scale(x_ref, r0, rc)
        for c0 in range(0, d, LANES):
            sl = slice(c0, c0 + LANES)
            t = (x_ref[r0:r0 + rc, sl] * inv * gain_ref[:, sl]) * (1.0 + mod_ref[0, 1:2, sl]) + mod_ref[0, 0:1, sl]
            h_ref[r0:r0 + rc, sl] = t.astype(h_ref.dtype)


def _post_into(o_ref, x_ref, gain_ref, mod_ref, weight):
    rows, d = x_ref.shape
    rc = min(ROW_CHUNK, rows)
    for r0 in range(0, rows, rc):
        inv = _row_rms_scale(o_ref, r0, rc)
        for c0 in range(0, d, LANES):
            sl = slice(c0, c0 + LANES)
            y = o_ref[r0:r0 + rc, sl]
            o_ref[r0:r0 + rc, sl] = (x_ref[r0:r0 + rc, sl]
                                     + (weight * (1.0 + mod_ref[0, 2:3, sl])) * (y * inv * gain_ref[:, sl]))


def _mod_kernel(c_ref, w_ref, b_ref, o_ref):
    c = c_ref[...]
    act = (c * _sigmoid(c)).astype(BF16)
    o_ref[0] = jnp.dot(act, w_ref[0].astype(BF16), preferred_element_type=F32) + b_ref[0]


def _modulation(c_pad, w_mod, b_mod):
    depth, d, width = w_mod.shape
    tn = 1024
    return pl.pallas_call(
        _mod_kernel,
        grid=(depth, width // tn),
        in_specs=[
            pl.BlockSpec((MOD_ROWS, d), lambda l, n: (0, 0)),
            pl.BlockSpec((1, d, tn), lambda l, n: (l, 0, n)),
            pl.BlockSpec((1, 1, tn), lambda l, n: (l, 0, n)),
        ],
        out_specs=pl.BlockSpec((1, MOD_ROWS, tn), lambda l, n: (l, 0, n)),
        out_shape=jax.ShapeDtypeStruct((depth, MOD_ROWS, width), F32),
        compiler_params=pltpu.CompilerParams(dimension_semantics=("parallel", "parallel")),
        name="modulation",
    )(c_pad, w_mod, b_mod.reshape(depth, 1, width))


def _ffn_kernel(x_ref, mod_ref, pre_g_ref, post_g_ref, wg_ref, wu_ref, wd_ref, o_ref, h_ref,
                *, n_chunks, weight):
    j = pl.program_id(1)

    @pl.when(j == 0)
    def _():
        _pre_into(h_ref, x_ref, pre_g_ref, mod_ref)
        o_ref[...] = jnp.zeros_like(o_ref)

    h = h_ref[...]
    tf = wg_ref.shape[1]
    halves = [slice(c0, c0 + tf // 2) for c0 in (0, tf // 2)]
    gu = [(jnp.dot(h, wg_ref[:, sl], preferred_element_type=F32),
           jnp.dot(h, wu_ref[:, sl], preferred_element_type=F32)) for sl in halves]
    down = None
    for sl, (g, u) in zip(halves, gu):
        a = ((g * _sigmoid(g)) * u).astype(BF16)
        part = jnp.dot(a, wd_ref[sl, :], preferred_element_type=F32)
        down = part if down is None else down + part
    o_ref[...] += down

    @pl.when(j == n_chunks - 1)
    def _():
        _post_into(o_ref, x_ref, post_g_ref, mod_ref, weight)


def _ffn_carry_kernel(x_ref, mod_ref, pre_g_ref, post_g_ref, wg_ref, wu_ref, wd_ref, carry_ref, o_ref, h_ref,
                      **kw):
    del carry_ref
    _ffn_kernel(x_ref, mod_ref, pre_g_ref, post_g_ref, wg_ref, wu_ref, wd_ref, o_ref, h_ref, **kw)


def _ffn(x, mod, pre_g, post_g, wg, wu, wd, *, layer, which, seq, weight, tm, tf,
         tiles=None, in_tile0=0, out_tile0=0, seq_tile0=0, out_rows=None, carry=None):
    d = x.shape[1]
    d_ff = wg.shape[-1]
    tiles = x.shape[0] // tm if tiles is None else tiles
    out_rows = x.shape[0] if out_rows is None else out_rows
    tiles_per_seq = seq // tm
    n_chunks = d_ff // tf
    in_specs = [
        pl.BlockSpec((tm, d), lambda r, j: (r + in_tile0, 0)),
        pl.BlockSpec((1, N_MOD, d), lambda r, j: ((r + seq_tile0) // tiles_per_seq, 0, 0)),
        pl.BlockSpec((1, d), lambda r, j: (0, 0)),
        pl.BlockSpec((1, d), lambda r, j: (0, 0)),
        pl.BlockSpec((None, None, d, tf), lambda r, j: (layer, which, 0, j)),
        pl.BlockSpec((None, None, d, tf), lambda r, j: (layer, which, 0, j)),
        pl.BlockSpec((None, None, tf, d), lambda r, j: (layer, which, j, 0)),
    ]
    args = [x, mod, pre_g.reshape(1, d), post_g.reshape(1, d), wg, wu, wd]
    body, aliases = _ffn_kernel, {}
    if carry is not None:
        in_specs.append(pl.BlockSpec(memory_space=pl.ANY))
        args.append(carry)
        body, aliases = _ffn_carry_kernel, {len(args) - 1: 0}
    return pl.pallas_call(
        functools.partial(body, n_chunks=n_chunks, weight=weight),
        grid=(tiles, n_chunks),
        in_specs=in_specs,
        out_specs=pl.BlockSpec((tm, d), lambda r, j: (r + out_tile0, 0)),
        out_shape=jax.ShapeDtypeStruct((out_rows, d), F32),
        scratch_shapes=[pltpu.VMEM((tm, d), BF16)],
        input_output_aliases=aliases,
        compiler_params=pltpu.CompilerParams(
            dimension_semantics=("parallel", "arbitrary"),
            vmem_limit_bytes=V7X_VMEM_LIMIT_BYTES),
        name="ffn",
    )(*args)


def _inproj_kernel(x_ref, mod_ref, pre_g_ref, w_ref, cos_ref, sin_ref, wc_ref,
                   q_ref, k_ref, v_ref, pq_ref, h_ref):
    _pre_into(h_ref, x_ref, pre_g_ref, mod_ref)
    proj = jnp.dot(h_ref[...], w_ref[...], preferred_element_type=F32)
    cos_t = cos_ref[...]
    sin_t = sin_ref[...]

    def rope(t):
        return t * cos_t + pltpu.roll(t, HEAD_DIM // 2, 1) * sin_t

    q_scale = LOG2_E * HEAD_DIM ** -0.5
    for hh in range(N_Q_HEADS):
        sl = slice(hh * HEAD_DIM, (hh + 1) * HEAD_DIM)
        q_ref[:, sl] = (rope(proj[:, sl]) * q_scale).astype(BF16)
    for hh in range(N_KV_HEADS):
        src = slice(ATTN_WIDTH + hh * HEAD_DIM, ATTN_WIDTH + (hh + 1) * HEAD_DIM)
        k_ref[:, hh * HEAD_DIM:(hh + 1) * HEAD_DIM] = rope(proj[:, src]).astype(BF16)
    v_ref[...] = proj[:, ATTN_WIDTH + KV_WIDTH:ATTN_WIDTH + 2 * KV_WIDTH].astype(BF16)
    u0 = ATTN_WIDTH + 2 * KV_WIDTH
    wc = wc_ref[...]
    for g in range(N_FOURIER_GROUPS):
        dst = slice(g * FOURIER_GROUP_DIM, (g + 1) * FOURIER_GROUP_DIM)
        ug = proj[:, u0 + g * FOURIER_GROUP_DIM:u0 + (g + 1) * FOURIER_GROUP_DIM].astype(BF16)
        pq = jnp.dot(ug, wc, preferred_element_type=F32)
        pq_ref[0, :, dst] = pq[:, :FOURIER_GROUP_DIM].astype(BF16)
        pq_ref[1, :, dst] = pq[:, FOURIER_GROUP_DIM:].astype(BF16)


def _inproj(x, mod, pre_g, w_in, cos_t, sin_t, wc, *, layer, seq, tm):
    rows, d = x.shape
    tiles_per_seq = seq // tm
    width = w_in.shape[-1]
    return pl.pallas_call(
        _inproj_kernel,
        grid=(rows // tm,),
        in_specs=[
            pl.BlockSpec((tm, d), lambda r: (r, 0)),
            pl.BlockSpec((1, N_MOD, d), lambda r: (r // tiles_per_seq, 0, 0)),
            pl.BlockSpec((1, d), lambda r: (0, 0)),
            pl.BlockSpec((None, d, width), lambda r: (layer, 0, 0)),
            pl.BlockSpec((tm, HEAD_DIM), lambda r: (r % tiles_per_seq, 0)),
            pl.BlockSpec((tm, HEAD_DIM), lambda r: (r % tiles_per_seq, 0)),
            pl.BlockSpec((FOURIER_GROUP_DIM, 2 * FOURIER_GROUP_DIM), lambda r: (0, 0)),
        ],
        out_specs=[
            pl.BlockSpec((tm, ATTN_WIDTH), lambda r: (r, 0)),
            pl.BlockSpec((tm, KV_WIDTH), lambda r: (r, 0)),
            pl.BlockSpec((tm, KV_WIDTH), lambda r: (r, 0)),
            pl.BlockSpec((2, tm, FOURIER_WIDTH), lambda r: (0, r, 0)),
        ],
        out_shape=[
            jax.ShapeDtypeStruct((rows, ATTN_WIDTH), BF16),
            jax.ShapeDtypeStruct((rows, KV_WIDTH), BF16),
            jax.ShapeDtypeStruct((rows, KV_WIDTH), BF16),
            jax.ShapeDtypeStruct((2, rows, FOURIER_WIDTH), BF16),
        ],
        scratch_shapes=[pltpu.VMEM((tm, d), BF16)],
        compiler_params=pltpu.CompilerParams(
            dimension_semantics=("parallel",), vmem_limit_bytes=V7X_VMEM_LIMIT_BYTES),
        name="inproj",
    )(x, mod, pre_g.reshape(1, d), w_in, cos_t, sin_t, wc)


def _attn_kernel(sink_ref, q_ref, kp_ref, kc_ref, kn_ref, vp_ref, vc_ref, vn_ref, g_ref,
                 o_ref, kbuf, vbuf, *, q_blocks, blocks_per_seq):
    tq = q_blocks * BLOCK
    band = 3 * BLOCK
    ext = 2 * HEAD_DIM
    kbuf[0:BLOCK] = kp_ref[...]
    kbuf[BLOCK:BLOCK + tq] = kc_ref[...]
    kbuf[BLOCK + tq:2 * BLOCK + tq] = kn_ref[...]
    for hk in range(N_KV_HEADS):
        src = slice(hk * HEAD_DIM, (hk + 1) * HEAD_DIM)
        dst = slice(hk * ext, hk * ext + HEAD_DIM)
        vbuf[0:BLOCK, dst] = vp_ref[:, src]
        vbuf[BLOCK:BLOCK + tq, dst] = vc_ref[:, src]
        vbuf[BLOCK + tq:2 * BLOCK + tq, dst] = vn_ref[:, src]
        vbuf[:, hk * ext + HEAD_DIM:(hk + 1) * ext] = jnp.ones((tq + 2 * BLOCK, HEAD_DIM), BF16)

    first_block = (pl.program_id(0) * q_blocks) % blocks_per_seq
    qi = lax.broadcasted_iota(jnp.int32, (BLOCK, BLOCK), 0)
    kj = lax.broadcasted_iota(jnp.int32, (BLOCK, BLOCK), 1)
    tri_prev = jnp.where(kj >= qi, 0.0, NEG_INF)
    tri_next = jnp.where(kj <= qi, 0.0, NEG_INF)
    gain = g_ref[...]
    n_chains = q_blocks * N_KV_HEADS

    def scores(c):
        b, hk = divmod(c, N_KV_HEADS)
        r0 = b * BLOCK
        qs = jnp.concatenate(
            [q_ref[r0:r0 + BLOCK, (hk * Q_PER_KV + g) * HEAD_DIM:(hk * Q_PER_KV + g + 1) * HEAD_DIM]
             for g in range(Q_PER_KV)], axis=0)
        kb = kbuf[r0:r0 + band, hk * HEAD_DIM:(hk + 1) * HEAD_DIM]
        return lax.dot_general(qs, kb, (((1,), (1,)), ((), ())), preferred_element_type=F32)

    def softmax(c, s):
        b, hk = divmod(c, N_KV_HEADS)
        n = first_block + b
        bias_prev = tri_prev + jnp.where(n == 0, NEG_INF, 0.0)
        bias_next = tri_next + jnp.where(n == blocks_per_seq - 1, NEG_INF, 0.0)
        ps, sink_terms = [], []
        for g in range(Q_PER_KV):
            sg = s[g * BLOCK:(g + 1) * BLOCK]
            s0 = sg[:, 0:BLOCK] + bias_prev
            s1 = sg[:, BLOCK:2 * BLOCK]
            s2 = sg[:, 2 * BLOCK:] + bias_next
            sink = sink_ref[hk * Q_PER_KV + g] * LOG2_E
            m = jnp.maximum(jnp.max(jnp.maximum(jnp.maximum(s0, s1), s2), axis=-1, keepdims=True), sink)
            ps.append(jnp.concatenate([jnp.exp2(s0 - m), jnp.exp2(s1 - m), jnp.exp2(s2 - m)],
                                      axis=1).astype(BF16))
            sink_terms.append(jnp.exp2(sink - m))
        return jnp.concatenate(ps, axis=0), sink_terms

    def weighted_values(c, p, sink_terms):
        b, hk = divmod(c, N_KV_HEADS)
        r0 = b * BLOCK
        oe = jnp.dot(p, vbuf[r0:r0 + band, hk * ext:(hk + 1) * ext], preferred_element_type=F32)
        outs = []
        for g in range(Q_PER_KV):
            og = oe[g * BLOCK:(g + 1) * BLOCK]
            denom = og[:, HEAD_DIM:] + sink_terms[g]
            outs.append(og[:, :HEAD_DIM] * (1.0 / denom))
        return outs

    def finish_block(b, heads):
        r0 = b * BLOCK
        sq = heads[0] * heads[0]
        for t in heads[1:]:
            sq = sq + t * t
        inv = lax.rsqrt(jnp.sum(sq, axis=-1, keepdims=True) * (1.0 / ATTN_WIDTH) + RMS_EPS)
        for hh, t in enumerate(heads):
            sl = slice(hh * HEAD_DIM, (hh + 1) * HEAD_DIM)
            o_ref[r0:r0 + BLOCK, sl] = (t * inv * gain[:, sl]).astype(BF16)

    lead = 2
    s_vals = {c: scores(c) for c in range(min(lead, n_chains))}
    p_vals, heads = {}, {}
    for c in range(n_chains):
        if c + lead < n_chains:
            s_vals[c + lead] = scores(c + lead)
        p_vals[c] = softmax(c, s_vals.pop(c))
        for done in ([c - 1] if c >= 1 else []) + ([c] if c == n_chains - 1 else []):
            b, hk = divmod(done, N_KV_HEADS)
            heads.setdefault(b, []).extend(weighted_values(done, *p_vals.pop(done)))
            if hk == N_KV_HEADS - 1:
                finish_block(b, heads.pop(b))


def _attention(q, k, v, sink, gain, *, seq, q_blocks):
    rows = q.shape[0]
    tq = q_blocks * BLOCK
    bps = seq // BLOCK

    def prev_map(r):
        g0 = r * q_blocks
        return (jnp.where(g0 % bps == 0, g0, g0 - 1), 0)

    def next_map(r):
        g1 = (r + 1) * q_blocks
        return (jnp.where(g1 % bps == 0, g1 - 1, g1), 0)

    edge = pl.BlockSpec((BLOCK, KV_WIDTH), prev_map)
    edge_n = pl.BlockSpec((BLOCK, KV_WIDTH), next_map)
    cur = pl.BlockSpec((tq, KV_WIDTH), lambda r: (r, 0))
    return pl.pallas_call(
        functools.partial(_attn_kernel, q_blocks=q_blocks, blocks_per_seq=bps),
        grid=(rows // tq,),
        in_specs=[
            pl.BlockSpec(memory_space=pltpu.SMEM),
            pl.BlockSpec((tq, ATTN_WIDTH), lambda r: (r, 0)),
            edge, cur, edge_n, edge, cur, edge_n,
            pl.BlockSpec((1, ATTN_WIDTH), lambda r: (0, 0)),
        ],
        out_specs=pl.BlockSpec((tq, ATTN_WIDTH), lambda r: (r, 0)),
        out_shape=jax.ShapeDtypeStruct((rows, ATTN_WIDTH), BF16),
        scratch_shapes=[pltpu.VMEM((tq + 2 * BLOCK, KV_WIDTH), BF16),
                        pltpu.VMEM((tq + 2 * BLOCK, 2 * KV_WIDTH), BF16)],
        compiler_params=pltpu.CompilerParams(dimension_semantics=("parallel",)),
        name="attention",
    )(sink, q, k, k, k, v, v, v, gain.reshape(1, ATTN_WIDTH))


def _fft_a_kernel(pq_ref, a_ref, tc_ref, ts_ref, o_ref, *, n_outer, cols):
    lw = pq_ref.shape[-1]
    x = pq_ref[...].reshape(2 * n_outer, lw)
    y = jnp.dot(a_ref[...], x, preferred_element_type=F32)
    reps = FOURIER_WIDTH // 128
    for t in range(cols):
        sl = slice(t * FOURIER_WIDTH, (t + 1) * FOURIER_WIDTH)
        yr = y[:n_outer, sl]
        yi = y[n_outer:, sl]
        tc = jnp.concatenate([tc_ref[t]] * reps, axis=1)
        ts = jnp.concatenate([ts_ref[t]] * reps, axis=1)
        o_ref[0, 0, :, sl] = (yr * tc + yi * ts).astype(BF16)
        o_ref[1, 0, :, sl] = (yi * tc - yr * ts).astype(BF16)


def _fft_a(pq, a_mat, tw_cos, tw_sin, *, batch, seq, cols):
    n_outer = seq // BLOCK
    lanes = BLOCK * FOURIER_WIDTH
    lw = cols * FOURIER_WIDTH
    pq4 = pq.reshape(2, batch, n_outer, lanes)
    out = pl.pallas_call(
        functools.partial(_fft_a_kernel, n_outer=n_outer, cols=cols),
        grid=(batch, BLOCK // cols),
        in_specs=[
            pl.BlockSpec((2, 1, n_outer, lw), lambda b, c: (0, b, 0, c)),
            pl.BlockSpec((2 * n_outer, 2 * n_outer), lambda b, c: (0, 0)),
            pl.BlockSpec((cols, n_outer, 128), lambda b, c: (c, 0, 0)),
            pl.BlockSpec((cols, n_outer, 128), lambda b, c: (c, 0, 0)),
        ],
        out_specs=pl.BlockSpec((2, 1, n_outer, lw), lambda b, c: (0, b, 0, c)),
        out_shape=jax.ShapeDtypeStruct((2, batch, n_outer, lanes), BF16),
        compiler_params=pltpu.CompilerParams(dimension_semantics=("parallel", "parallel")),
        name="fft_a",
    )(pq4, a_mat, tw_cos, tw_sin)
    return out.reshape(2, batch * seq, FOURIER_WIDTH)


def _fft_b_kernel(y_ref, c_ref, s_ref, wl_ref, g_ref, o_ref, *, k1_per_step):
    cm = c_ref[...]
    sm = s_ref[...]
    zs = []
    for r in range(k1_per_step):
        rs = slice(r * BLOCK, (r + 1) * BLOCK)
        z = (jnp.dot(cm, y_ref[0, rs, :], preferred_element_type=F32)
             + jnp.dot(sm, y_ref[1, rs, :], preferred_element_type=F32))
        zs.append(z.astype(BF16))
    z_all = jnp.concatenate(zs, axis=0) if k1_per_step > 1 else zs[0]
    outs = []
    for g in range(N_FOURIER_GROUPS):
        sl = slice(g * FOURIER_GROUP_DIM, (g + 1) * FOURIER_GROUP_DIM)
        outs.append(jnp.dot(z_all[:, sl], wl_ref[g], preferred_element_type=F32))
    sq = outs[0] * outs[0]
    for t in outs[1:]:
        sq = sq + t * t
    inv = lax.rsqrt(jnp.sum(sq, axis=-1, keepdims=True) * (1.0 / FOURIER_WIDTH) + RMS_EPS)
    gain = g_ref[...]
    for g, t in enumerate(outs):
        sl = slice(g * FOURIER_GROUP_DIM, (g + 1) * FOURIER_GROUP_DIM)
        normed = (t * inv * gain[:, sl]).astype(BF16)
        for r in range(k1_per_step):
            o_ref[0, :, r * FOURIER_WIDTH + g * FOURIER_GROUP_DIM:
                  r * FOURIER_WIDTH + (g + 1) * FOURIER_GROUP_DIM] = normed[r * BLOCK:(r + 1) * BLOCK]


def _fft_b(yy, c_mat, s_mat, w_lin, gain, *, batch, seq, k1_per_step):
    n_outer = seq // BLOCK
    steps = n_outer // k1_per_step
    out = pl.pallas_call(
        functools.partial(_fft_b_kernel, k1_per_step=k1_per_step),
        grid=(batch, steps),
        in_specs=[
            pl.BlockSpec((2, k1_per_step * BLOCK, FOURIER_WIDTH), lambda b, i: (0, b * steps + i, 0)),
            pl.BlockSpec((BLOCK, BLOCK), lambda b, i: (0, 0)),
            pl.BlockSpec((BLOCK, BLOCK), lambda b, i: (0, 0)),
            pl.BlockSpec((N_FOURIER_GROUPS, FOURIER_GROUP_DIM, FOURIER_GROUP_DIM), lambda b, i: (0, 0, 0)),
            pl.BlockSpec((1, FOURIER_WIDTH), lambda b, i: (0, 0)),
        ],
        out_specs=pl.BlockSpec((1, BLOCK, k1_per_step * FOURIER_WIDTH), lambda b, i: (b, 0, i)),
        out_shape=jax.ShapeDtypeStruct((batch, BLOCK, n_outer * FOURIER_WIDTH), BF16),
        compiler_params=pltpu.CompilerParams(dimension_semantics=("parallel", "parallel")),
        name="fft_b",
    )(yy, c_mat, s_mat, w_lin, gain.reshape(1, FOURIER_WIDTH))
    return out.reshape(batch * seq, FOURIER_WIDTH)


def _outproj_kernel(a_ref, f_ref, x_ref, mod_ref, post_g_ref, w_ref, o_ref):
    o_ref[...] = (jnp.dot(a_ref[...], w_ref[:ATTN_WIDTH, :], preferred_element_type=F32)
                  + jnp.dot(f_ref[...], w_ref[ATTN_WIDTH:, :], preferred_element_type=F32))
    _post_into(o_ref, x_ref, post_g_ref, mod_ref, 1.0)


def _outproj(a, f, x, mod, post_g, w_out, *, layer, seq, tm):
    rows, d = x.shape
    tiles_per_seq = seq // tm
    return pl.pallas_call(
        _outproj_kernel,
        grid=(rows // tm,),
        in_specs=[
            pl.BlockSpec((tm, ATTN_WIDTH), lambda r: (r, 0)),
            pl.BlockSpec((tm, FOURIER_WIDTH), lambda r: (r, 0)),
            pl.BlockSpec((tm, d), lambda r: (r, 0)),
            pl.BlockSpec((1, N_MOD, d), lambda r: (r // tiles_per_seq, 0, 0)),
            pl.BlockSpec((1, d), lambda r: (0, 0)),
            pl.BlockSpec((None, ATTN_WIDTH + FOURIER_WIDTH, d), lambda r: (layer, 0, 0)),
        ],
        out_specs=pl.BlockSpec((tm, d), lambda r: (r, 0)),
        out_shape=jax.ShapeDtypeStruct((rows, d), F32),
        compiler_params=pltpu.CompilerParams(
            dimension_semantics=("parallel",), vmem_limit_bytes=V7X_VMEM_LIMIT_BYTES),
        name="outproj",
    )(a, f, x, mod, post_g.reshape(1, d), w_out)


def _rope_tables(seq):
    inv_freq = ROPE_THETA ** (-jnp.arange(0, HEAD_DIM, 2, dtype=F32) / HEAD_DIM)
    ang = jnp.arange(seq, dtype=F32)[:, None] * inv_freq[None, :]
    cos, sin = jnp.cos(ang), jnp.sin(ang)
    return jnp.concatenate([cos, cos], axis=-1), jnp.concatenate([-sin, sin], axis=-1)


def _dft_tables(seq):
    n_outer = seq // BLOCK

    def cs(n, scale):
        idx = np.arange(n)
        ang = 2.0 * np.pi * ((idx[:, None] * idx[None, :]) % n) / n
        return np.cos(ang) * scale, np.sin(ang) * scale

    cc, sc = cs(FOURIER_GROUP_DIM, FOURIER_GROUP_DIM ** -0.5)
    wc = np.concatenate([cc, sc], axis=1)
    co, so = cs(n_outer, n_outer ** -0.5)
    a_mat = np.block([[co, -so], [-so, -co]])
    c128, s128 = cs(BLOCK, BLOCK ** -0.5)
    n2 = np.arange(BLOCK)[:, None]
    k1 = np.arange(n_outer)[None, :]
    tw = 2.0 * np.pi * ((n2 * k1) % seq) / seq
    tw_cos = np.broadcast_to(np.cos(tw)[:, :, None], (BLOCK, n_outer, 128))
    tw_sin = np.broadcast_to(np.sin(tw)[:, :, None], (BLOCK, n_outer, 128))
    as_f32 = lambda a: jnp.asarray(np.ascontiguousarray(a), dtype=F32)
    return (as_f32(wc).astype(BF16), as_f32(a_mat).astype(BF16), as_f32(tw_cos), as_f32(tw_sin),
            as_f32(c128).astype(BF16), as_f32(s128).astype(BF16))


def _tile(seq, want):
    return min(seq, want)


def _trunk(x_groups, c_groups, w_mod, b_mod, pre_g, post_g, ffn_w_gate, ffn_w_up, ffn_w_down,
           w_in, attn_sink, fourier_w, branch_g, w_out):
    seq, d = x_groups[0].shape[1:]
    sizes = [x.shape[0] for x in x_groups]
    batch = sum(sizes)
    depth = w_mod.shape[0]
    assert seq % BLOCK == 0 and batch <= MOD_ROWS and all(x.shape[1:] == (seq, d) for x in x_groups)
    n_outer = seq // BLOCK

    c_pad = jnp.zeros((MOD_ROWS, d), F32).at[:batch].set(jnp.concatenate(c_groups, axis=0))
    mod = _modulation(c_pad, w_mod, b_mod).reshape(depth, MOD_ROWS, N_SUBLAYERS, N_MOD, d)

    cos_t, sin_t = _rope_tables(seq)
    wc, a_mat, tw_cos, tw_sin, c128, s128 = _dft_tables(seq)

    wg = ffn_w_gate.astype(BF16)
    wu = ffn_w_up.astype(BF16)
    wd = ffn_w_down.astype(BF16)
    w_in_b = w_in.astype(BF16)
    w_out_b = w_out.astype(BF16)
    w_lin = fourier_w.astype(BF16)

    tm_ffn = _tile(seq, 1024)
    tf = 512 if wg.shape[-1] % 512 == 0 else wg.shape[-1]
    tm_proj = _tile(seq, 512)
    q_blocks = min(4, n_outer)
    fft_cols = 8
    k1_per_step = min(4, n_outer)
    tiles_per_seq = seq // tm_ffn
    group_tile0 = [sum(sizes[:i]) * tiles_per_seq for i in range(len(sizes))]

    def ffn(x, l, sub, which, **kw):
        return _ffn(x, mod[l, :, sub], pre_g[l, sub], post_g[l, sub], wg, wu, wd, layer=l, which=which,
                    seq=seq, weight=0.5, tm=tm_ffn, tf=tf, **kw)

    xs = None
    for l in range(depth):
        if l == 0:
            for gi in reversed(range(len(sizes))):
                xs = ffn(x_groups[gi].reshape(sizes[gi] * seq, d), l, 0, 0,
                         tiles=sizes[gi] * tiles_per_seq, out_tile0=group_tile0[gi], seq_tile0=group_tile0[gi],
                         out_rows=batch * seq, carry=xs)
        else:
            xs = ffn(xs, l, 0, 0)
        q, k, v, pq = _inproj(xs, mod[l, :, 1], pre_g[l, 1], w_in_b, cos_t, sin_t, wc,
                              layer=l, seq=seq, tm=tm_proj)
        a_out = _attention(q, k, v, attn_sink[l], branch_g[l, 0], seq=seq, q_blocks=q_blocks)
        yy = _fft_a(pq, a_mat, tw_cos, tw_sin, batch=batch, seq=seq, cols=fft_cols)
        f_out = _fft_b(yy, c128, s128, w_lin[l], branch_g[l, 1], batch=batch, seq=seq,
                       k1_per_step=k1_per_step)
        xs = _outproj(a_out, f_out, xs, mod[l, :, 1], post_g[l, 1], w_out_b, layer=l, seq=seq, tm=tm_proj)
        if l < depth - 1:
            xs = ffn(xs, l, 2, 1)
    outs = []
    for gi, n in enumerate(sizes):
        y = ffn(xs, depth - 1, 2, 1, tiles=n * tiles_per_seq, in_tile0=group_tile0[gi],
                seq_tile0=group_tile0[gi], out_rows=n * seq)
        outs.append(y.reshape(n, seq, d))
    return outs


def kernel(x_prompt, x_sample, c_prompt, c_sample, w_mod, b_mod, pre_g, post_g, ffn_w_gate, ffn_w_up,
           ffn_w_down, w_in, attn_sink, fourier_w, branch_g, w_out):
    y_prompt, y_sample = _trunk([x_prompt, x_sample], [c_prompt, c_sample], w_mod, b_mod, pre_g, post_g,
                                ffn_w_gate, ffn_w_up, ffn_w_down, w_in, attn_sink, fourier_w, branch_g, w_out)
    return y_prompt, y_sample
```

```python
import functools

import numpy as np
import jax
import jax.numpy as jnp
from jax import lax
from jax.experimental import pallas as pl
from jax.experimental.pallas import tpu as pltpu

HEAD_DIM = 128
N_Q_HEADS = 8
N_KV_HEADS = 2
Q_PER_KV = N_Q_HEADS // N_KV_HEADS
ATTN_WIDTH = N_Q_HEADS * HEAD_DIM
KV_WIDTH = N_KV_HEADS * HEAD_DIM
N_FOURIER_GROUPS = 8
FOURIER_GROUP_DIM = 128
FOURIER_WIDTH = N_FOURIER_GROUPS * FOURIER_GROUP_DIM
WINDOW = 128
BLOCK = 128
ROPE_THETA = 10000.0
N_SUBLAYERS = 3
N_MOD = 3
RMS_EPS = 1e-6
NEG_INF = -1e30
LOG2_E = 1.4426950408889634
MOD_ROWS = 8

V7X_VMEM_LIMIT_BYTES = 60 * 1024 * 1024

BF16 = jnp.bfloat16
F32 = jnp.float32


def _sigmoid(x):
    return 1.0 / (1.0 + jnp.exp(-x))


LANES = 128
ROW_CHUNK = 128


def _row_rms_scale(ref, r0, rows):
    d = ref.shape[-1]
    acc = None
    for c0 in range(0, d, LANES):
        t = ref[r0:r0 + rows, c0:c0 + LANES]
        acc = t * t if acc is None else acc + t * t
    return lax.rsqrt(jnp.sum(acc, axis=-1, keepdims=True) * (1.0 / d) + RMS_EPS)


def _pre_into(h_ref, x_ref, gain_ref, mod_ref):
    rows, d = x_ref.shape
    rc = min(ROW_CHUNK, rows)
    for r0 in range(0, rows, rc):
        inv = _row_rms_scale(x_ref, r0, rc)
        for c0 in range(0, d, LANES):
            sl = slice(c0, c0 + LANES)
            t = (x_ref[r0:r0 + rc, sl] * inv * gain_ref[:, sl]) * (1.0 + mod_ref[0, 1:2, sl]) + mod_ref[0, 0:1, sl]
            h_ref[r0:r0 + rc, sl] = t.astype(h_ref.dtype)


def _post_into(o_ref, x_ref, gain_ref, mod_ref, weight):
    rows, d = x_ref.shape
    rc = min(ROW_CHUNK, rows)
    for r0 in range(0, rows, rc):
        inv = _row_rms_scale(o_ref, r0, rc)
        for c0 in range(0, d, LANES):
            sl = slice(c0, c0 + LANES)
            y = o_ref[r0:r0 + rc, sl]
            o_ref[r0:r0 + rc, sl] = (x_ref[r0:r0 + rc, sl]
                                     + (weight * (1.0 + mod_ref[0, 2:3, sl])) * (y * inv * gain_ref[:, sl]))


def _mod_kernel(c_ref, w_ref, b_ref, o_ref):
    c = c_ref[...]
    act = (c * _sigmoid(c)).astype(BF16)
    o_ref[0] = jnp.dot(act, w_ref[0].astype(BF16), preferred_element_type=F32) + b_ref[0]


def _modulation(c_pad, w_mod, b_mod):
    depth, d, width = w_mod.shape
    tn = 1024
    return pl.pallas_call(
        _mod_kernel,
        grid=(depth, width // tn),
        in_specs=[
            pl.BlockSpec((MOD_ROWS, d), lambda l, n: (0, 0)),
            pl.BlockSpec((1, d, tn), lambda l, n: (l, 0, n)),
            pl.BlockSpec((1, 1, tn), lambda l, n: (l, 0, n)),
        ],
        out_specs=pl.BlockSpec((1, MOD_ROWS, tn), lambda l, n: (l, 0, n)),
        out_shape=jax.ShapeDtypeStruct((depth, MOD_ROWS, width), F32),
        compiler_params=pltpu.CompilerParams(dimension_semantics=("parallel", "parallel")),
        name="modulation",
    )(c_pad, w_mod, b_mod.reshape(depth, 1, width))


def _ffn_kernel(x_ref, mod_ref, pre_g_ref, post_g_ref, wg_ref, wu_ref, wd_ref, o_ref, h_ref,
                *, n_chunks, weight):
    j = pl.program_id(1)

    @pl.when(j == 0)
    def _():
        _pre_into(h_ref, x_ref, pre_g_ref, mod_ref)
        o_ref[...] = jnp.zeros_like(o_ref)

    h = h_ref[...]
    tf = wg_ref.shape[1]
    halves = [slice(c0, c0 + tf // 2) for c0 in (0, tf // 2)]
    gu = [(jnp.dot(h, wg_ref[:, sl], preferred_element_type=F32),
           jnp.dot(h, wu_ref[:, sl], preferred_element_type=F32)) for sl in halves]
    down = None
    for sl, (g, u) in zip(halves, gu):
        a = ((g * _sigmoid(g)) * u).astype(BF16)
        part = jnp.dot(a, wd_ref[sl, :], preferred_element_type=F32)
        down = part if down is None else down + part
    o_ref[...] += down

    @pl.when(j == n_chunks - 1)
    def _():
        _post_into(o_ref, x_ref, post_g_ref, mod_ref, weight)


def _ffn_carry_kernel(x_ref, mod_ref, pre_g_ref, post_g_ref, wg_ref, wu_ref, wd_ref, carry_ref, o_ref, h_ref,
                      **kw):
    del carry_ref
    _ffn_kernel(x_ref, mod_ref, pre_g_ref, post_g_ref, wg_ref, wu_ref, wd_ref, o_ref, h_ref, **kw)


def _ffn(x, mod, pre_g, post_g, wg, wu, wd, *, layer, which, seq, weight, tm, tf,
         tiles=None, in_tile0=0, out_tile0=0, seq_tile0=0, out_rows=None, carry=None):
    d = x.shape[1]
    d_ff = wg.shape[-1]
    tiles = x.shape[0] // tm if tiles is None else tiles
    out_rows = x.shape[0] if out_rows is None else out_rows
    tiles_per_seq = seq // tm
    n_chunks = d_ff // tf
    in_specs = [
        pl.BlockSpec((tm, d), lambda r, j: (r + in_tile0, 0)),
        pl.BlockSpec((1, N_MOD, d), lambda r, j: ((r + seq_tile0) // tiles_per_seq, 0, 0)),
        pl.BlockSpec((1, d), lambda r, j: (0, 0)),
        pl.BlockSpec((1, d), lambda r, j: (0, 0)),
        pl.BlockSpec((None, None, d, tf), lambda r, j: (layer, which, 0, j)),
        pl.BlockSpec((None, None, d, tf), lambda r, j: (layer, which, 0, j)),
        pl.BlockSpec((None, None, tf, d), lambda r, j: (layer, which, j, 0)),
    ]
    args = [x, mod, pre_g.reshape(1, d), post_g.reshape(1, d), wg, wu, wd]
    body, aliases = _ffn_kernel, {}
    if carry is not None:
        in_specs.append(pl.BlockSpec(memory_space=pl.ANY))
        args.append(carry)
        body, aliases = _ffn_carry_kernel, {len(args) - 1: 0}
    return pl.pallas_call(
        functools.partial(body, n_chunks=n_chunks, weight=weight),
        grid=(tiles, n_chunks),
        in_specs=in_specs,
        out_specs=pl.BlockSpec((tm, d), lambda r, j: (r + out_tile0, 0)),
        out_shape=jax.ShapeDtypeStruct((out_rows, d), F32),
        scratch_shapes=[pltpu.VMEM((tm, d), BF16)],
        input_output_aliases=aliases,
        compiler_params=pltpu.CompilerParams(
            dimension_semantics=("parallel", "arbitrary"),
            vmem_limit_bytes=V7X_VMEM_LIMIT_BYTES),
        name="ffn",
    )(*args)


def _inproj_kernel(x_ref, mod_ref, pre_g_ref, w_ref, cos_ref, sin_ref, wc_ref,
                   q_ref, k_ref, v_ref, pq_ref, h_ref):
    _pre_into(h_ref, x_ref, pre_g_ref, mod_ref)
    proj = jnp.dot(h_ref[...], w_ref[...], preferred_element_type=F32)
    cos_t = cos_ref[...]
    sin_t = sin_ref[...]

    def rope(t):
        return t * cos_t + pltpu.roll(t, HEAD_DIM // 2, 1) * sin_t

    q_scale = LOG2_E * HEAD_DIM ** -0.5
    for hh in range(N_Q_HEADS):
        sl = slice(hh * HEAD_DIM, (hh + 1) * HEAD_DIM)
        q_ref[:, sl] = (rope(proj[:, sl]) * q_scale).astype(BF16)
    for hh in range(N_KV_HEADS):
        src = slice(ATTN_WIDTH + hh * HEAD_DIM, ATTN_WIDTH + (hh + 1) * HEAD_DIM)
        k_ref[:, hh * HEAD_DIM:(hh + 1) * HEAD_DIM] = rope(proj[:, src]).astype(BF16)
    v_ref[...] = proj[:, ATTN_WIDTH + KV_WIDTH:ATTN_WIDTH + 2 * KV_WIDTH].astype(BF16)
    u0 = ATTN_WIDTH + 2 * KV_WIDTH
    wc = wc_ref[...]
    for g in range(N_FOURIER_GROUPS):
        dst = slice(g * FOURIER_GROUP_DIM, (g + 1) * FOURIER_GROUP_DIM)
        ug = proj[:, u0 + g * FOURIER_GROUP_DIM:u0 + (g + 1) * FOURIER_GROUP_DIM].astype(BF16)
        pq = jnp.dot(ug, wc, preferred_element_type=F32)
        pq_ref[0, :, dst] = pq[:, :FOURIER_GROUP_DIM].astype(BF16)
        pq_ref[1, :, dst] = pq[:, FOURIER_GROUP_DIM:].astype(BF16)


def _inproj(x, mod, pre_g, w_in, cos_t, sin_t, wc, *, layer, seq, tm):
    rows, d = x.shape
    tiles_per_seq = seq // tm
    width = w_in.shape[-1]
    return pl.pallas_call(
        _inproj_kernel,
        grid=(rows // tm,),
        in_specs=[
            pl.BlockSpec((tm, d), lambda r: (r, 0)),
            pl.BlockSpec((1, N_MOD, d), lambda r: (r // tiles_per_seq, 0, 0)),
            pl.BlockSpec((1, d), lambda r: (0, 0)),
            pl.BlockSpec((None, d, width), lambda r: (layer, 0, 0)),
            pl.BlockSpec((tm, HEAD_DIM), lambda r: (r % tiles_per_seq, 0)),
            pl.BlockSpec((tm, HEAD_DIM), lambda r: (r % tiles_per_seq, 0)),
            pl.BlockSpec((FOURIER_GROUP_DIM, 2 * FOURIER_GROUP_DIM), lambda r: (0, 0)),
        ],
        out_specs=[
            pl.BlockSpec((tm, ATTN_WIDTH), lambda r: (r, 0)),
            pl.BlockSpec((tm, KV_WIDTH), lambda r: (r, 0)),
            pl.BlockSpec((tm, KV_WIDTH), lambda r: (r, 0)),
            pl.BlockSpec((2, tm, FOURIER_WIDTH), lambda r: (0, r, 0)),
        ],
        out_shape=[
            jax.ShapeDtypeStruct((rows, ATTN_WIDTH), BF16),
            jax.ShapeDtypeStruct((rows, KV_WIDTH), BF16),
            jax.ShapeDtypeStruct((rows, KV_WIDTH), BF16),
            jax.ShapeDtypeStruct((2, rows, FOURIER_WIDTH), BF16),
        ],
        scratch_shapes=[pltpu.VMEM((tm, d), BF16)],
        compiler_params=pltpu.CompilerParams(
            dimension_semantics=("parallel",), vmem_limit_bytes=V7X_VMEM_LIMIT_BYTES),
        name="inproj",
    )(x, mod, pre_g.reshape(1, d), w_in, cos_t, sin_t, wc)


def _attn_kernel(sink_ref, q_ref, kp_ref, kc_ref, kn_ref, vp_ref, vc_ref, vn_ref, g_ref,
                 o_ref, kbuf, vbuf, *, q_blocks, blocks_per_seq):
    tq = q_blocks * BLOCK
    band = 3 * BLOCK
    ext = 2 * HEAD_DIM
    kbuf[0:BLOCK] = kp_ref[...]
    kbuf[BLOCK:BLOCK + tq] = kc_ref[...]
    kbuf[BLOCK + tq:2 * BLOCK + tq] = kn_ref[...]
    for hk in range(N_KV_HEADS):
        src = slice(hk * HEAD_DIM, (hk + 1) * HEAD_DIM)
        dst = slice(hk * ext, hk * ext + HEAD_DIM)
        vbuf[0:BLOCK, dst] = vp_ref[:, src]
        vbuf[BLOCK:BLOCK + tq, dst] = vc_ref[:, src]
        vbuf[BLOCK + tq:2 * BLOCK + tq, dst] = vn_ref[:, src]
        vbuf[:, hk * ext + HEAD_DIM:(hk + 1) * ext] = jnp.ones((tq + 2 * BLOCK, HEAD_DIM), BF16)

    first_block = (pl.program_id(0) * q_blocks) % blocks_per_seq
    qi = lax.broadcasted_iota(jnp.int32, (BLOCK, BLOCK), 0)
    kj = lax.broadcasted_iota(jnp.int32, (BLOCK, BLOCK), 1)
    tri_prev = jnp.where(kj >= qi, 0.0, NEG_INF)
    tri_next = jnp.where(kj <= qi, 0.0, NEG_INF)
    gain = g_ref[...]
    n_chains = q_blocks * N_KV_HEADS

    def scores(c):
        b, hk = divmod(c, N_KV_HEADS)
        r0 = b * BLOCK
        qs = jnp.concatenate(
            [q_ref[r0:r0 + BLOCK, (hk * Q_PER_KV + g) * HEAD_DIM:(hk * Q_PER_KV + g + 1) * HEAD_DIM]
             for g in range(Q_PER_KV)], axis=0)
        kb = kbuf[r0:r0 + band, hk * HEAD_DIM:(hk + 1) * HEAD_DIM]
        return lax.dot_general(qs, kb, (((1,), (1,)), ((), ())), preferred_element_type=F32)

    def softmax(c, s):
        b, hk = divmod(c, N_KV_HEADS)
        n = first_block + b
        bias_prev = tri_prev + jnp.where(n == 0, NEG_INF, 0.0)
        bias_next = tri_next + jnp.where(n == blocks_per_seq - 1, NEG_INF, 0.0)
        ps, sink_terms = [], []
        for g in range(Q_PER_KV):
            sg = s[g * BLOCK:(g + 1) * BLOCK]
            s0 = sg[:, 0:BLOCK] + bias_prev
            s1 = sg[:, BLOCK:2 * BLOCK]
            s2 = sg[:, 2 * BLOCK:] + bias_next
            sink = sink_ref[hk * Q_PER_KV + g] * LOG2_E
            m = jnp.maximum(jnp.max(jnp.maximum(jnp.maximum(s0, s1), s2), axis=-1, keepdims=True), sink)
            ps.append(jnp.concatenate([jnp.exp2(s0 - m), jnp.exp2(s1 - m), jnp.exp2(s2 - m)],
                                      axis=1).astype(BF16))
            sink_terms.append(jnp.exp2(sink - m))
        return jnp.concatenate(ps, axis=0), sink_terms

    def weighted_values(c, p, sink_terms):
        b, hk = divmod(c, N_KV_HEADS)
        r0 = b * BLOCK
        oe = jnp.dot(p, vbuf[r0:r0 + band, hk * ext:(hk + 1) * ext], preferred_element_type=F32)
        outs = []
        for g in range(Q_PER_KV):
            og = oe[g * BLOCK:(g + 1) * BLOCK]
            denom = og[:, HEAD_DIM:] + sink_terms[g]
            outs.append(og[:, :HEAD_DIM] * (1.0 / denom))
        return outs

    def finish_block(b, heads):
        r0 = b * BLOCK
        sq = heads[0] * heads[0]
        for t in heads[1:]:
            sq = sq + t * t
        inv = lax.rsqrt(jnp.sum(sq, axis=-1, keepdims=True) * (1.0 / ATTN_WIDTH) + RMS_EPS)
        for hh, t in enumerate(heads):
            sl = slice(hh * HEAD_DIM, (hh + 1) * HEAD_DIM)
            o_ref[r0:r0 + BLOCK, sl] = (t * inv * gain[:, sl]).astype(BF16)

    lead = 2
    s_vals = {c: scores(c) for c in range(min(lead, n_chains))}
    p_vals, heads = {}, {}
    for c in range(n_chains):
        if c + lead < n_chains:
            s_vals[c + lead] = scores(c + lead)
        p_vals[c] = softmax(c, s_vals.pop(c))
        for done in ([c - 1] if c >= 1 else []) + ([c] if c == n_chains - 1 else []):
            b, hk = divmod(done, N_KV_HEADS)
            heads.setdefault(b, []).extend(weighted_values(done, *p_vals.pop(done)))
            if hk == N_KV_HEADS - 1:
                finish_block(b, heads.pop(b))


def _attention(q, k, v, sink, gain, *, seq, q_blocks):
    rows = q.shape[0]
    tq = q_blocks * BLOCK
    bps = seq // BLOCK

    def prev_map(r):
        g0 = r * q_blocks
        return (jnp.where(g0 % bps == 0, g0, g0 - 1), 0)

    def next_map(r):
        g1 = (r + 1) * q_blocks
        return (jnp.where(g1 % bps == 0, g1 - 1, g1), 0)

    edge = pl.BlockSpec((BLOCK, KV_WIDTH), prev_map)
    edge_n = pl.BlockSpec((BLOCK, KV_WIDTH), next_map)
    cur = pl.BlockSpec((tq, KV_WIDTH), lambda r: (r, 0))
    return pl.pallas_call(
        functools.partial(_attn_kernel, q_blocks=q_blocks, blocks_per_seq=bps),
        grid=(rows // tq,),
        in_specs=[
            pl.BlockSpec(memory_space=pltpu.SMEM),
            pl.BlockSpec((tq, ATTN_WIDTH), lambda r: (r, 0)),
            edge, cur, edge_n, edge, cur, edge_n,
            pl.BlockSpec((1, ATTN_WIDTH), lambda r: (0, 0)),
        ],
        out_specs=pl.BlockSpec((tq, ATTN_WIDTH), lambda r: (r, 0)),
        out_shape=jax.ShapeDtypeStruct((rows, ATTN_WIDTH), BF16),
        scratch_shapes=[pltpu.VMEM((tq + 2 * BLOCK, KV_WIDTH), BF16),
                        pltpu.VMEM((tq + 2 * BLOCK, 2 * KV_WIDTH), BF16)],
        compiler_params=pltpu.CompilerParams(dimension_semantics=("parallel",)),
        name="attention",
    )(sink, q, k, k, k, v, v, v, gain.reshape(1, ATTN_WIDTH))


FFT_COLS = 8


def _pitch(rows):
    return rows + 8 if (rows // 8) % 2 == 0 else rows


def _fft_kernel(pq_ref, a_ref, tc_ref, ts_ref, c_ref, s_ref, wl_ref, o_ref, p_s, q_s, z_s, *, n_outer):
    in_pitch = _pitch(BLOCK)
    out_pitch = _pitch(n_outer)
    for n1 in range(n_outer):
        rows = slice(n1 * BLOCK, (n1 + 1) * BLOCK)
        p_s[n1 * in_pitch:n1 * in_pitch + BLOCK, :] = pq_ref[0, rows, :].astype(F32)
        q_s[n1 * in_pitch:n1 * in_pitch + BLOCK, :] = pq_ref[1, rows, :].astype(F32)

    def slow_rows(n2):
        return pl.ds(n2, n_outer, stride=in_pitch)

    a_mat = a_ref[...]
    for n2_0 in range(0, BLOCK, FFT_COLS):
        cols = range(n2_0, n2_0 + FFT_COLS)
        x = jnp.concatenate([jnp.concatenate([p_s[slow_rows(n2), :] for n2 in cols], axis=1),
                             jnp.concatenate([q_s[slow_rows(n2), :] for n2 in cols], axis=1)], axis=0)
        y = jnp.dot(a_mat, x.astype(BF16), preferred_element_type=F32)
        for t, n2 in enumerate(cols):
            yr = y[:n_outer, t * LANES:(t + 1) * LANES]
            yi = y[n_outer:, t * LANES:(t + 1) * LANES]
            tc = tc_ref[n2]
            ts = ts_ref[n2]
            p_s[slow_rows(n2), :] = yr * tc + yi * ts
            q_s[slow_rows(n2), :] = yi * tc - yr * ts

    c_mat = c_ref[...]
    s_mat = s_ref[...]
    w_lin = wl_ref[...]
    step = min(FFT_COLS, n_outer)
    for k1_0 in range(0, n_outer, step):
        ks = range(k1_0, k1_0 + step)
        yr = jnp.concatenate([p_s[k1 * in_pitch:k1 * in_pitch + BLOCK, :] for k1 in ks], axis=1).astype(BF16)
        yi = jnp.concatenate([q_s[k1 * in_pitch:k1 * in_pitch + BLOCK, :] for k1 in ks], axis=1).astype(BF16)
        z = (jnp.dot(c_mat, yr, preferred_element_type=F32)
             + jnp.dot(s_mat, yi, preferred_element_type=F32))
        z_rows = jnp.concatenate([z[:, t * LANES:(t + 1) * LANES] for t in range(step)], axis=0).astype(BF16)
        out = jnp.dot(z_rows, w_lin, preferred_element_type=F32)
        for t, k1 in enumerate(ks):
            z_s[pl.ds(k1, BLOCK, stride=out_pitch), :] = out[t * BLOCK:(t + 1) * BLOCK]
    for k2 in range(BLOCK):
        o_ref[k2 * n_outer:(k2 + 1) * n_outer, :] = z_s[k2 * out_pitch:k2 * out_pitch + n_outer, :].astype(o_ref.dtype)


def _fourier_mix(pq, a_mat, tw_cos, tw_sin, c_mat, s_mat, w_lin, *, layer, batch, seq):
    n_outer = seq // BLOCK
    const = functools.partial(pl.BlockSpec, pipeline_mode=pl.Buffered(1))
    return pl.pallas_call(
        functools.partial(_fft_kernel, n_outer=n_outer),
        grid=(batch, N_FOURIER_GROUPS),
        in_specs=[
            pl.BlockSpec((2, seq, LANES), lambda b, g: (0, b, g)),
            const((2 * n_outer, 2 * n_outer), lambda b, g: (0, 0)),
            const((BLOCK, n_outer, LANES), lambda b, g: (0, 0, 0)),
            const((BLOCK, n_outer, LANES), lambda b, g: (0, 0, 0)),
            const((BLOCK, BLOCK), lambda b, g: (0, 0)),
            const((BLOCK, BLOCK), lambda b, g: (0, 0)),
            pl.BlockSpec((None, None, FOURIER_GROUP_DIM, FOURIER_GROUP_DIM), lambda b, g: (layer, g, 0, 0)),
        ],
        out_specs=pl.BlockSpec((seq, LANES), lambda b, g: (b, g)),
        out_shape=jax.ShapeDtypeStruct((batch * seq, FOURIER_WIDTH), BF16),
        scratch_shapes=[pltpu.VMEM((n_outer * _pitch(BLOCK), LANES), F32),
                        pltpu.VMEM((n_outer * _pitch(BLOCK), LANES), F32),
                        pltpu.VMEM((BLOCK * _pitch(n_outer), LANES), F32)],
        compiler_params=pltpu.CompilerParams(
            dimension_semantics=("parallel", "parallel"), vmem_limit_bytes=V7X_VMEM_LIMIT_BYTES),
        name="fourier_mix",
    )(pq, a_mat, tw_cos, tw_sin, c_mat, s_mat, w_lin)


def _outproj_kernel(a_ref, f_ref, x_ref, mod_ref, post_g_ref, fg_ref, w_ref, o_ref, fn_ref):
    rows = f_ref.shape[0]
    rc = min(ROW_CHUNK, rows)
    for r0 in range(0, rows, rc):
        acc = None
        for c0 in range(0, FOURIER_WIDTH, LANES):
            t = f_ref[r0:r0 + rc, c0:c0 + LANES].astype(F32)
            acc = t * t if acc is None else acc + t * t
        inv = lax.rsqrt(jnp.sum(acc, axis=-1, keepdims=True) * (1.0 / FOURIER_WIDTH) + RMS_EPS)
        for c0 in range(0, FOURIER_WIDTH, LANES):
            sl = slice(c0, c0 + LANES)
            fn_ref[r0:r0 + rc, sl] = (f_ref[r0:r0 + rc, sl].astype(F32) * inv * fg_ref[:, sl]).astype(BF16)
    o_ref[...] = (jnp.dot(a_ref[...], w_ref[:ATTN_WIDTH, :], preferred_element_type=F32)
                  + jnp.dot(fn_ref[...], w_ref[ATTN_WIDTH:, :], preferred_element_type=F32))
    _post_into(o_ref, x_ref, post_g_ref, mod_ref, 1.0)


def _outproj(a, f, x, mod, post_g, f_gain, w_out, *, layer, seq, tm):
    rows, d = x.shape
    tiles_per_seq = seq // tm
    return pl.pallas_call(
        _outproj_kernel,
        grid=(rows // tm,),
        in_specs=[
            pl.BlockSpec((tm, ATTN_WIDTH), lambda r: (r, 0)),
            pl.BlockSpec((tm, FOURIER_WIDTH), lambda r: (r, 0)),
            pl.BlockSpec((tm, d), lambda r: (r, 0)),
            pl.BlockSpec((1, N_MOD, d), lambda r: (r // tiles_per_seq, 0, 0)),
            pl.BlockSpec((1, d), lambda r: (0, 0)),
            pl.BlockSpec((1, FOURIER_WIDTH), lambda r: (0, 0)),
            pl.BlockSpec((None, ATTN_WIDTH + FOURIER_WIDTH, d), lambda r: (layer, 0, 0)),
        ],
        out_specs=pl.BlockSpec((tm, d), lambda r: (r, 0)),
        out_shape=jax.ShapeDtypeStruct((rows, d), F32),
        scratch_shapes=[pltpu.VMEM((tm, FOURIER_WIDTH), BF16)],
        compiler_params=pltpu.CompilerParams(
            dimension_semantics=("parallel",), vmem_limit_bytes=V7X_VMEM_LIMIT_BYTES),
        name="outproj",
    )(a, f, x, mod, post_g.reshape(1, d), f_gain.reshape(1, FOURIER_WIDTH), w_out)


def _rope_tables(seq):
    inv_freq = ROPE_THETA ** (-jnp.arange(0, HEAD_DIM, 2, dtype=F32) / HEAD_DIM)
    ang = jnp.arange(seq, dtype=F32)[:, None] * inv_freq[None, :]
    cos, sin = jnp.cos(ang), jnp.sin(ang)
    return jnp.concatenate([cos, cos], axis=-1), jnp.concatenate([-sin, sin], axis=-1)


def _dft_tables(seq):
    n_outer = seq // BLOCK

    def cs(n, scale):
        idx = np.arange(n)
        ang = 2.0 * np.pi * ((idx[:, None] * idx[None, :]) % n) / n
        return np.cos(ang) * scale, np.sin(ang) * scale

    cc, sc = cs(FOURIER_GROUP_DIM, FOURIER_GROUP_DIM ** -0.5)
    wc = np.concatenate([cc, sc], axis=1)
    co, so = cs(n_outer, n_outer ** -0.5)
    a_mat = np.block([[co, -so], [-so, -co]])
    c128, s128 = cs(BLOCK, BLOCK ** -0.5)
    n2 = np.arange(BLOCK)[:, None]
    k1 = np.arange(n_outer)[None, :]
    tw = 2.0 * np.pi * ((n2 * k1) % seq) / seq
    tw_cos = np.broadcast_to(np.cos(tw)[:, :, None], (BLOCK, n_outer, 128))
    tw_sin = np.broadcast_to(np.sin(tw)[:, :, None], (BLOCK, n_outer, 128))
    as_f32 = lambda a: jnp.asarray(np.ascontiguousarray(a), dtype=F32)
    return (as_f32(wc).astype(BF16), as_f32(a_mat).astype(BF16), as_f32(tw_cos), as_f32(tw_sin),
            as_f32(c128).astype(BF16), as_f32(s128).astype(BF16))


def _tile(seq, want):
    return min(seq, want)


def _trunk(x_groups, c_groups, w_mod, b_mod, pre_g, post_g, ffn_w_gate, ffn_w_up, ffn_w_down,
           w_in, attn_sink, fourier_w, branch_g, w_out):
    seq, d = x_groups[0].shape[1:]
    sizes = [x.shape[0] for x in x_groups]
    batch = sum(sizes)
    depth = w_mod.shape[0]
    assert seq % BLOCK == 0 and batch <= MOD_ROWS and all(x.shape[1:] == (seq, d) for x in x_groups)
    n_outer = seq // BLOCK

    c_pad = jnp.zeros((MOD_ROWS, d), F32).at[:batch].set(jnp.concatenate(c_groups, axis=0))
    mod = _modulation(c_pad, w_mod, b_mod).reshape(depth, MOD_ROWS, N_SUBLAYERS, N_MOD, d)

    cos_t, sin_t = _rope_tables(seq)
    wc, a_mat, tw_cos, tw_sin, c128, s128 = _dft_tables(seq)

    wg = ffn_w_gate.astype(BF16)
    wu = ffn_w_up.astype(BF16)
    wd = ffn_w_down.astype(BF16)
    w_in_b = w_in.astype(BF16)
    w_out_b = w_out.astype(BF16)
    w_lin = fourier_w.astype(BF16)

    tm_ffn = _tile(seq, 1024)
    tf = 512 if wg.shape[-1] % 512 == 0 else wg.shape[-1]
    tm_proj = _tile(seq, 512)
    q_blocks = min(4, n_outer)
    tiles_per_seq = seq // tm_ffn
    group_tile0 = [sum(sizes[:i]) * tiles_per_seq for i in range(len(sizes))]

    def ffn(x, l, sub, which, **kw):
        return _ffn(x, mod[l, :, sub], pre_g[l, sub], post_g[l, sub], wg, wu, wd, layer=l, which=which,
                    seq=seq, weight=0.5, tm=tm_ffn, tf=tf, **kw)

    xs = None
    for l in range(depth):
        if l == 0:
            for gi in reversed(range(len(sizes))):
                xs = ffn(x_groups[gi].reshape(sizes[gi] * seq, d), l, 0, 0,
                         tiles=sizes[gi] * tiles_per_seq, out_tile0=group_tile0[gi], seq_tile0=group_tile0[gi],
                         out_rows=batch * seq, carry=xs)
        else:
            xs = ffn(xs, l, 0, 0)
        q, k, v, pq = _inproj(xs, mod[l, :, 1], pre_g[l, 1], w_in_b, cos_t, sin_t, wc,
                              layer=l, seq=seq, tm=tm_proj)
        a_out = _attention(q, k, v, attn_sink[l], branch_g[l, 0], seq=seq, q_blocks=q_blocks)
        f_raw = _fourier_mix(pq, a_mat, tw_cos, tw_sin, c128, s128, w_lin, layer=l, batch=batch, seq=seq)
        xs = _outproj(a_out, f_raw, xs, mod[l, :, 1], post_g[l, 1], branch_g[l, 1], w_out_b,
                      layer=l, seq=seq, tm=tm_proj)
        if l < depth - 1:
            xs = ffn(xs, l, 2, 1)
    outs = []
    for gi, n in enumerate(sizes):
        y = ffn(xs, depth - 1, 2, 1, tiles=n * tiles_per_seq, in_tile0=group_tile0[gi],
                seq_tile0=group_tile0[gi], out_rows=n * seq)
        outs.append(y.reshape(n, seq, d))
    return outs


def kernel(x_prompt, x_sample, c_prompt, c_sample, w_mod, b_mod, pre_g, post_g, ffn_w_gate, ffn_w_up,
           ffn_w_down, w_in, attn_sink, fourier_w, branch_g, w_out):
    y_prompt, y_sample = _trunk([x_prompt, x_sample], [c_prompt, c_sample], w_mod, b_mod, pre_g, post_g,
                                ffn_w_gate, ffn_w_up, ffn_w_down, w_in, attn_sink, fourier_w, branch_g, w_out)
    return y_prompt, y_sample
```

```python
import functools

import numpy as np
import jax
import jax.numpy as jnp
from jax import lax
from jax.experimental import pallas as pl
from jax.experimental.pallas import tpu as pltpu

HEAD_DIM = 128
N_Q_HEADS = 8
N_KV_HEADS = 2
Q_PER_KV = N_Q_HEADS // N_KV_HEADS
ATTN_WIDTH = N_Q_HEADS * HEAD_DIM
KV_WIDTH = N_KV_HEADS * HEAD_DIM
N_FOURIER_GROUPS = 8
FOURIER_GROUP_DIM = 128
FOURIER_WIDTH = N_FOURIER_GROUPS * FOURIER_GROUP_DIM
WINDOW = 128
BLOCK = 128
ROPE_THETA = 10000.0
N_SUBLAYERS = 3
N_MOD = 3
RMS_EPS = 1e-6
NEG_INF = -1e30
LOG2_E = 1.4426950408889634
MOD_ROWS = 8

V7X_VMEM_LIMIT_BYTES = 60 * 1024 * 1024

BF16 = jnp.bfloat16
F32 = jnp.float32


def _sigmoid(x):
    return 1.0 / (1.0 + jnp.exp(-x))


LANES = 128
ROW_CHUNK = 128


def _row_rms_scale(ref, r0, rows):
    d = ref.shape[-1]
    acc = None
    for c0 in range(0, d, LANES):
        t = ref[r0:r0 + rows, c0:c0 + LANES]
        acc = t * t if acc is None else acc + t * t
    return lax.rsqrt(jnp.sum(acc, axis=-1, keepdims=True) * (1.0 / d) + RMS_EPS)


def _pre_into(h_ref, x_ref, gain_ref, mod_ref):
    rows, d = x_ref.shape
    rc = min(ROW_CHUNK, rows)
    for r0 in range(0, rows, rc):
        inv = _row_rms_scale(x_ref, r0, rc)
        for c0 in range(0, d, LANES):
            sl = slice(c0, c0 + LANES)
            t = (x_ref[r0:r0 + rc, sl] * inv * gain_ref[:, sl]) * (1.0 + mod_ref[0, 1:2, sl]) + mod_ref[0, 0:1, sl]
            h_ref[r0:r0 + rc, sl] = t.astype(h_ref.dtype)


def _post_into(o_ref, x_ref, gain_ref, mod_ref, weight):
    rows, d = x_ref.shape
    rc = min(ROW_CHUNK, rows)
    for r0 in range(0, rows, rc):
        inv = _row_rms_scale(o_ref, r0, rc)
        for c0 in range(0, d, LANES):
            sl = slice(c0, c0 + LANES)
            y = o_ref[r0:r0 + rc, sl]
            o_ref[r0:r0 + rc, sl] = (x_ref[r0:r0 + rc, sl]
                                     + (weight * (1.0 + mod_ref[0, 2:3, sl])) * (y * inv * gain_ref[:, sl]))


def _mod_kernel(c_ref, w_ref, b_ref, o_ref):
    c = c_ref[...]
    act = (c * _sigmoid(c)).astype(BF16)
    o_ref[0] = jnp.dot(act, w_ref[0].astype(BF16), preferred_element_type=F32) + b_ref[0]


def _modulation(c_pad, w_mod, b_mod):
    depth, d, width = w_mod.shape
    tn = 1024
    return pl.pallas_call(
        _mod_kernel,
        grid=(depth, width // tn),
        in_specs=[
            pl.BlockSpec((MOD_ROWS, d), lambda l, n: (0, 0)),
            pl.BlockSpec((1, d, tn), lambda l, n: (l, 0, n)),
            pl.BlockSpec((1, 1, tn), lambda l, n: (l, 0, n)),
        ],
        out_specs=pl.BlockSpec((1, MOD_ROWS, tn), lambda l, n: (l, 0, n)),
        out_shape=jax.ShapeDtypeStruct((depth, MOD_ROWS, width), F32),
        compiler_params=pltpu.CompilerParams(dimension_semantics=("parallel", "parallel")),
        name="modulation",
    )(c_pad, w_mod, b_mod.reshape(depth, 1, width))


def _ffn_kernel(x_ref, mod_ref, pre_g_ref, post_g_ref, wg_hbm, wu_hbm, wd_hbm, o_ref,
                h_ref, wg_buf, wu_buf, wd_buf, sem, *, n_chunks, weight, layer, which, active_tiles):
    tf = wg_buf.shape[2]

    def chunk_copies(c, slot):
        cols = pl.ds(pl.multiple_of(c * tf, tf), tf)
        return (pltpu.make_async_copy(wg_hbm.at[layer, which, :, cols], wg_buf.at[slot], sem.at[0, slot]),
                pltpu.make_async_copy(wu_hbm.at[layer, which, :, cols], wu_buf.at[slot], sem.at[1, slot]),
                pltpu.make_async_copy(wd_hbm.at[layer, which, cols, :], wd_buf.at[slot], sem.at[2, slot]))

    def chunk(c, carry):
        slot = c % 2
        for cp in chunk_copies(c, slot):
            cp.wait()

        @pl.when(c + 1 < n_chunks)
        def _():
            for cp in chunk_copies(c + 1, 1 - slot):
                cp.start()

        h = h_ref[...]
        halves = [slice(c0, c0 + tf // 2) for c0 in (0, tf // 2)]
        gu = [(jnp.dot(h, wg_buf[slot, :, sl], preferred_element_type=F32),
               jnp.dot(h, wu_buf[slot, :, sl], preferred_element_type=F32)) for sl in halves]
        down = None
        for sl, (g, u) in zip(halves, gu):
            a = ((g * _sigmoid(g)) * u).astype(BF16)
            part = jnp.dot(a, wd_buf[slot, sl, :], preferred_element_type=F32)
            down = part if down is None else down + part
        o_ref[...] += down
        return carry

    def row_tile():
        for cp in chunk_copies(0, 0):
            cp.start()
        _pre_into(h_ref, x_ref, pre_g_ref, mod_ref)
        o_ref[...] = jnp.zeros_like(o_ref)
        lax.fori_loop(0, n_chunks, chunk, 0)
        _post_into(o_ref, x_ref, post_g_ref, mod_ref, weight)

    if active_tiles is None:
        row_tile()
    else:
        lo, hi = active_tiles
        r = pl.program_id(0)
        active = (r >= lo) & (r < hi)
        pl.when(active)(row_tile)

        @pl.when(jnp.logical_not(active))
        def _():
            o_ref[...] = jnp.zeros_like(o_ref)


def _ffn_carry_kernel(x_ref, mod_ref, pre_g_ref, post_g_ref, wg_hbm, wu_hbm, wd_hbm, carry_ref, o_ref,
                      h_ref, wg_buf, wu_buf, wd_buf, sem, **kw):
    del carry_ref
    _ffn_kernel(x_ref, mod_ref, pre_g_ref, post_g_ref, wg_hbm, wu_hbm, wd_hbm, o_ref,
                h_ref, wg_buf, wu_buf, wd_buf, sem, **kw)


def _ffn(x, mod, pre_g, post_g, wg, wu, wd, *, layer, which, seq, weight, tm, tf,
         tiles=None, in_tile0=0, out_tile0=0, seq_tile0=0, out_rows=None, carry=None):
    d = x.shape[1]
    d_ff = wg.shape[-1]
    tiles = x.shape[0] // tm if tiles is None else tiles
    out_rows = x.shape[0] if out_rows is None else out_rows
    tiles_per_seq = seq // tm
    n_chunks = d_ff // tf
    if carry is None:
        grid_tiles, first = out_rows // tm, out_tile0
    else:
        grid_tiles, first = tiles, 0
    active = None if grid_tiles == tiles else (first, first + tiles)

    def local(r):
        return jnp.clip(r - first, 0, tiles - 1)

    hbm = pl.BlockSpec(memory_space=pl.ANY)
    in_specs = [
        pl.BlockSpec((tm, d), lambda r: (local(r) + in_tile0, 0)),
        pl.BlockSpec((1, N_MOD, d), lambda r: ((local(r) + seq_tile0) // tiles_per_seq, 0, 0)),
        pl.BlockSpec((1, d), lambda r: (0, 0)),
        pl.BlockSpec((1, d), lambda r: (0, 0)),
        hbm, hbm, hbm,
    ]
    args = [x, mod, pre_g.reshape(1, d), post_g.reshape(1, d), wg, wu, wd]
    body, aliases = _ffn_kernel, {}
    if carry is not None:
        in_specs.append(hbm)
        args.append(carry)
        body, aliases = _ffn_carry_kernel, {len(args) - 1: 0}
    return pl.pallas_call(
        functools.partial(body, n_chunks=n_chunks, weight=weight, layer=layer, which=which, active_tiles=active),
        grid=(grid_tiles,),
        in_specs=in_specs,
        out_specs=pl.BlockSpec((tm, d), lambda r: (r - first + out_tile0, 0)),
        out_shape=jax.ShapeDtypeStruct((out_rows, d), F32),
        scratch_shapes=[pltpu.VMEM((tm, d), BF16),
                        pltpu.VMEM((2, d, tf), BF16), pltpu.VMEM((2, d, tf), BF16), pltpu.VMEM((2, tf, d), BF16),
                        pltpu.SemaphoreType.DMA((3, 2))],
        input_output_aliases=aliases,
        compiler_params=pltpu.CompilerParams(
            dimension_semantics=("parallel",),
            vmem_limit_bytes=V7X_VMEM_LIMIT_BYTES),
        name="ffn",
    )(*args)


def _inproj_kernel(x_ref, mod_ref, pre_g_ref, w_ref, cos_ref, sin_ref, wc_ref,
                   q_ref, k_ref, v_ref, pq_ref, h_ref):
    _pre_into(h_ref, x_ref, pre_g_ref, mod_ref)
    proj = jnp.dot(h_ref[...], w_ref[...], preferred_element_type=F32)
    cos_t = cos_ref[...]
    sin_t = sin_ref[...]

    def rope(t):
        return t * cos_t + pltpu.roll(t, HEAD_DIM // 2, 1) * sin_t

    q_scale = LOG2_E * HEAD_DIM ** -0.5
    for hh in range(N_Q_HEADS):
        sl = slice(hh * HEAD_DIM, (hh + 1) * HEAD_DIM)
        q_ref[:, sl] = (rope(proj[:, sl]) * q_scale).astype(BF16)
    for hh in range(N_KV_HEADS):
        src = slice(ATTN_WIDTH + hh * HEAD_DIM, ATTN_WIDTH + (hh + 1) * HEAD_DIM)
        k_ref[:, hh * HEAD_DIM:(hh + 1) * HEAD_DIM] = rope(proj[:, src]).astype(BF16)
    v_ref[...] = proj[:, ATTN_WIDTH + KV_WIDTH:ATTN_WIDTH + 2 * KV_WIDTH].astype(BF16)
    u0 = ATTN_WIDTH + 2 * KV_WIDTH
    wc = wc_ref[...]
    for g in range(N_FOURIER_GROUPS):
        dst = slice(g * FOURIER_GROUP_DIM, (g + 1) * FOURIER_GROUP_DIM)
        ug = proj[:, u0 + g * FOURIER_GROUP_DIM:u0 + (g + 1) * FOURIER_GROUP_DIM].astype(BF16)
        pq = jnp.dot(ug, wc, preferred_element_type=F32)
        pq_ref[0, :, dst] = pq[:, :FOURIER_GROUP_DIM].astype(BF16)
        pq_ref[1, :, dst] = pq[:, FOURIER_GROUP_DIM:].astype(BF16)


def _inproj(x, mod, pre_g, w_in, cos_t, sin_t, wc, *, layer, seq, tm):
    rows, d = x.shape
    tiles_per_seq = seq // tm
    width = w_in.shape[-1]
    return pl.pallas_call(
        _inproj_kernel,
        grid=(rows // tm,),
        in_specs=[
            pl.BlockSpec((tm, d), lambda r: (r, 0)),
            pl.BlockSpec((1, N_MOD, d), lambda r: (r // tiles_per_seq, 0, 0)),
            pl.BlockSpec((1, d), lambda r: (0, 0)),
            pl.BlockSpec((None, d, width), lambda r: (layer, 0, 0)),
            pl.BlockSpec((tm, HEAD_DIM), lambda r: (r % tiles_per_seq, 0)),
            pl.BlockSpec((tm, HEAD_DIM), lambda r: (r % tiles_per_seq, 0)),
            pl.BlockSpec((FOURIER_GROUP_DIM, 2 * FOURIER_GROUP_DIM), lambda r: (0, 0)),
        ],
        out_specs=[
            pl.BlockSpec((tm, ATTN_WIDTH), lambda r: (r, 0)),
            pl.BlockSpec((tm, KV_WIDTH), lambda r: (r, 0)),
            pl.BlockSpec((tm, KV_WIDTH), lambda r: (r, 0)),
            pl.BlockSpec((2, tm, FOURIER_WIDTH), lambda r: (0, r, 0)),
        ],
        out_shape=[
            jax.ShapeDtypeStruct((rows, ATTN_WIDTH), BF16),
            jax.ShapeDtypeStruct((rows, KV_WIDTH), BF16),
            jax.ShapeDtypeStruct((rows, KV_WIDTH), BF16),
            jax.ShapeDtypeStruct((2, rows, FOURIER_WIDTH), BF16),
        ],
        scratch_shapes=[pltpu.VMEM((tm, d), BF16)],
        compiler_params=pltpu.CompilerParams(
            dimension_semantics=("parallel",), vmem_limit_bytes=V7X_VMEM_LIMIT_BYTES),
        name="inproj",
    )(x, mod, pre_g.reshape(1, d), w_in, cos_t, sin_t, wc)


def _attn_kernel(sink_ref, q_ref, kp_ref, kc_ref, kn_ref, vp_ref, vc_ref, vn_ref, g_ref,
                 o_ref, kbuf, vbuf, *, q_blocks, blocks_per_seq):
    tq = q_blocks * BLOCK
    band = 3 * BLOCK
    ext = 2 * HEAD_DIM
    kbuf[0:BLOCK] = kp_ref[...]
    kbuf[BLOCK:BLOCK + tq] = kc_ref[...]
    kbuf[BLOCK + tq:2 * BLOCK + tq] = kn_ref[...]
    for hk in range(N_KV_HEADS):
        src = slice(hk * HEAD_DIM, (hk + 1) * HEAD_DIM)
        dst = slice(hk * ext, hk * ext + HEAD_DIM)
        vbuf[0:BLOCK, dst] = vp_ref[:, src]
        vbuf[BLOCK:BLOCK + tq, dst] = vc_ref[:, src]
        vbuf[BLOCK + tq:2 * BLOCK + tq, dst] = vn_ref[:, src]
        vbuf[:, hk * ext + HEAD_DIM:(hk + 1) * ext] = jnp.ones((tq + 2 * BLOCK, HEAD_DIM), BF16)

    first_block = (pl.program_id(0) * q_blocks) % blocks_per_seq
    qi = lax.broadcasted_iota(jnp.int32, (BLOCK, BLOCK), 0)
    kj = lax.broadcasted_iota(jnp.int32, (BLOCK, BLOCK), 1)
    tri_prev = jnp.where(kj >= qi, 0.0, NEG_INF)
    tri_next = jnp.where(kj <= qi, 0.0, NEG_INF)
    gain = g_ref[...]
    n_chains = q_blocks * N_KV_HEADS

    def scores(c):
        b, hk = divmod(c, N_KV_HEADS)
        r0 = b * BLOCK
        qs = jnp.concatenate(
            [q_ref[r0:r0 + BLOCK, (hk * Q_PER_KV + g) * HEAD_DIM:(hk * Q_PER_KV + g + 1) * HEAD_DIM]
             for g in range(Q_PER_KV)], axis=0)
        kb = kbuf[r0:r0 + band, hk * HEAD_DIM:(hk + 1) * HEAD_DIM]
        return lax.dot_general(qs, kb, (((1,), (1,)), ((), ())), preferred_element_type=F32)

    def softmax(c, s):
        b, hk = divmod(c, N_KV_HEADS)
        n = first_block + b
        bias_prev = tri_prev + jnp.where(n == 0, NEG_INF, 0.0)
        bias_next = tri_next + jnp.where(n == blocks_per_seq - 1, NEG_INF, 0.0)
        ps, sink_terms = [], []
        for g in range(Q_PER_KV):
            sg = s[g * BLOCK:(g + 1) * BLOCK]
            s0 = sg[:, 0:BLOCK] + bias_prev
            s1 = sg[:, BLOCK:2 * BLOCK]
            s2 = sg[:, 2 * BLOCK:] + bias_next
            sink = sink_ref[hk * Q_PER_KV + g] * LOG2_E
            m = jnp.maximum(jnp.max(jnp.maximum(jnp.maximum(s0, s1), s2), axis=-1, keepdims=True), sink)
            ps.append(jnp.concatenate([jnp.exp2(s0 - m), jnp.exp2(s1 - m), jnp.exp2(s2 - m)],
                                      axis=1).astype(BF16))
            sink_terms.append(jnp.exp2(sink - m))
        return jnp.concatenate(ps, axis=0), sink_terms

    def weighted_values(c, p, sink_terms):
        b, hk = divmod(c, N_KV_HEADS)
        r0 = b * BLOCK
        oe = jnp.dot(p, vbuf[r0:r0 + band, hk * ext:(hk + 1) * ext], preferred_element_type=F32)
        outs = []
        for g in range(Q_PER_KV):
            og = oe[g * BLOCK:(g + 1) * BLOCK]
            denom = og[:, HEAD_DIM:] + sink_terms[g]
            outs.append(og[:, :HEAD_DIM] * (1.0 / denom))
        return outs

    def finish_block(b, heads):
        r0 = b * BLOCK
        sq = heads[0] * heads[0]
        for t in heads[1:]:
            sq = sq + t * t
        inv = lax.rsqrt(jnp.sum(sq, axis=-1, keepdims=True) * (1.0 / ATTN_WIDTH) + RMS_EPS)
        for hh, t in enumerate(heads):
            sl = slice(hh * HEAD_DIM, (hh + 1) * HEAD_DIM)
            o_ref[r0:r0 + BLOCK, sl] = (t * inv * gain[:, sl]).astype(BF16)

    lead = 2
    s_vals = {c: scores(c) for c in range(min(lead, n_chains))}
    p_vals, heads = {}, {}
    for c in range(n_chains):
        if c + lead < n_chains:
            s_vals[c + lead] = scores(c + lead)
        p_vals[c] = softmax(c, s_vals.pop(c))
        for done in ([c - 1] if c >= 1 else []) + ([c] if c == n_chains - 1 else []):
            b, hk = divmod(done, N_KV_HEADS)
            heads.setdefault(b, []).extend(weighted_values(done, *p_vals.pop(done)))
            if hk == N_KV_HEADS - 1:
                finish_block(b, heads.pop(b))


def _attention(q, k, v, sink, gain, *, seq, q_blocks):
    rows = q.shape[0]
    tq = q_blocks * BLOCK
    bps = seq // BLOCK

    def prev_map(r):
        g0 = r * q_blocks
        return (jnp.where(g0 % bps == 0, g0, g0 - 1), 0)

    def next_map(r):
        g1 = (r + 1) * q_blocks
        return (jnp.where(g1 % bps == 0, g1 - 1, g1), 0)

    edge = pl.BlockSpec((BLOCK, KV_WIDTH), prev_map)
    edge_n = pl.BlockSpec((BLOCK, KV_WIDTH), next_map)
    cur = pl.BlockSpec((tq, KV_WIDTH), lambda r: (r, 0))
    return pl.pallas_call(
        functools.partial(_attn_kernel, q_blocks=q_blocks, blocks_per_seq=bps),
        grid=(rows // tq,),
        in_specs=[
            pl.BlockSpec(memory_space=pltpu.SMEM),
            pl.BlockSpec((tq, ATTN_WIDTH), lambda r: (r, 0)),
            edge, cur, edge_n, edge, cur, edge_n,
            pl.BlockSpec((1, ATTN_WIDTH), lambda r: (0, 0)),
        ],
        out_specs=pl.BlockSpec((tq, ATTN_WIDTH), lambda r: (r, 0)),
        out_shape=jax.ShapeDtypeStruct((rows, ATTN_WIDTH), BF16),
        scratch_shapes=[pltpu.VMEM((tq + 2 * BLOCK, KV_WIDTH), BF16),
                        pltpu.VMEM((tq + 2 * BLOCK, 2 * KV_WIDTH), BF16)],
        compiler_params=pltpu.CompilerParams(dimension_semantics=("parallel",)),
        name="attention",
    )(sink, q, k, k, k, v, v, v, gain.reshape(1, ATTN_WIDTH))


FFT_COLS = 8


SUBLANES = 8


def _pitch(rows):
    return rows + SUBLANES if (rows // SUBLANES) % 2 == 0 else rows


def _fft_kernel(pq_ref, a_ref, tc_ref, ts_ref, c_ref, s_ref, wl_ref, o_ref, p_s, q_s, z_s, *, n_outer):
    in_pitch = _pitch(BLOCK)
    out_pitch = _pitch(n_outer)
    for n1 in range(n_outer):
        rows = slice(n1 * BLOCK, (n1 + 1) * BLOCK)
        p_s[n1 * in_pitch:n1 * in_pitch + BLOCK, :] = pq_ref[0, rows, :].astype(F32)
        q_s[n1 * in_pitch:n1 * in_pitch + BLOCK, :] = pq_ref[1, rows, :].astype(F32)

    def slow_rows(n2):
        return pl.ds(n2, n_outer, stride=in_pitch)

    a_mat = a_ref[...]
    for n2_0 in range(0, BLOCK, FFT_COLS):
        cols = range(n2_0, n2_0 + FFT_COLS)
        x = jnp.concatenate([jnp.concatenate([p_s[slow_rows(n2), :] for n2 in cols], axis=1),
                             jnp.concatenate([q_s[slow_rows(n2), :] for n2 in cols], axis=1)], axis=0)
        y = jnp.dot(a_mat, x.astype(BF16), preferred_element_type=F32)
        for t, n2 in enumerate(cols):
            yr = y[:n_outer, t * LANES:(t + 1) * LANES]
            yi = y[n_outer:, t * LANES:(t + 1) * LANES]
            tc = tc_ref[n2]
            ts = ts_ref[n2]
            p_s[slow_rows(n2), :] = yr * tc + yi * ts
            q_s[slow_rows(n2), :] = yi * tc - yr * ts

    c_mat = c_ref[...]
    s_mat = s_ref[...]
    w_lin = wl_ref[...]
    step = min(FFT_COLS, n_outer)
    for k1_0 in range(0, n_outer, step):
        ks = range(k1_0, k1_0 + step)
        yr = jnp.concatenate([p_s[k1 * in_pitch:k1 * in_pitch + BLOCK, :] for k1 in ks], axis=1).astype(BF16)
        yi = jnp.concatenate([q_s[k1 * in_pitch:k1 * in_pitch + BLOCK, :] for k1 in ks], axis=1).astype(BF16)
        z = (jnp.dot(c_mat, yr, preferred_element_type=F32)
             + jnp.dot(s_mat, yi, preferred_element_type=F32))
        z_rows = jnp.concatenate([z[:, t * LANES:(t + 1) * LANES] for t in range(step)], axis=0).astype(BF16)
        out = jnp.dot(z_rows, w_lin, preferred_element_type=F32)
        for t, k1 in enumerate(ks):
            z_s[pl.ds(k1, BLOCK, stride=out_pitch), :] = out[t * BLOCK:(t + 1) * BLOCK]
    for k2 in range(BLOCK):
        o_ref[k2 * n_outer:(k2 + 1) * n_outer, :] = z_s[k2 * out_pitch:k2 * out_pitch + n_outer, :].astype(o_ref.dtype)


def _fourier_mix(pq, a_mat, tw_cos, tw_sin, c_mat, s_mat, w_lin, *, layer, batch, seq):
    n_outer = seq // BLOCK
    const = functools.partial(pl.BlockSpec, pipeline_mode=pl.Buffered(1))
    return pl.pallas_call(
        functools.partial(_fft_kernel, n_outer=n_outer),
        grid=(batch, N_FOURIER_GROUPS),
        in_specs=[
            pl.BlockSpec((2, seq, LANES), lambda b, g: (0, b, g)),
            const((2 * n_outer, 2 * n_outer), lambda b, g: (0, 0)),
            const((BLOCK, n_outer, LANES), lambda b, g: (0, 0, 0)),
            const((BLOCK, n_outer, LANES), lambda b, g: (0, 0, 0)),
            const((BLOCK, BLOCK), lambda b, g: (0, 0)),
            const((BLOCK, BLOCK), lambda b, g: (0, 0)),
            pl.BlockSpec((None, None, FOURIER_GROUP_DIM, FOURIER_GROUP_DIM), lambda b, g: (layer, g, 0, 0)),
        ],
        out_specs=pl.BlockSpec((seq, LANES), lambda b, g: (b, g)),
        out_shape=jax.ShapeDtypeStruct((batch * seq, FOURIER_WIDTH), BF16),
        scratch_shapes=[pltpu.VMEM((n_outer * _pitch(BLOCK), LANES), F32),
                        pltpu.VMEM((n_outer * _pitch(BLOCK), LANES), F32),
                        pltpu.VMEM((BLOCK * _pitch(n_outer), LANES), F32)],
        compiler_params=pltpu.CompilerParams(
            dimension_semantics=("parallel", "parallel"), vmem_limit_bytes=V7X_VMEM_LIMIT_BYTES),
        name="fourier_mix",
    )(pq, a_mat, tw_cos, tw_sin, c_mat, s_mat, w_lin)


def _outproj_kernel(a_ref, f_ref, x_ref, mod_ref, post_g_ref, fg_ref, w_ref, o_ref, fn_ref):
    rows = f_ref.shape[0]
    rc = min(ROW_CHUNK, rows)
    for r0 in range(0, rows, rc):
        acc = None
        for c0 in range(0, FOURIER_WIDTH, LANES):
            t = f_ref[r0:r0 + rc, c0:c0 + LANES].astype(F32)
            acc = t * t if acc is None else acc + t * t
        inv = lax.rsqrt(jnp.sum(acc, axis=-1, keepdims=True) * (1.0 / FOURIER_WIDTH) + RMS_EPS)
        for c0 in range(0, FOURIER_WIDTH, LANES):
            sl = slice(c0, c0 + LANES)
            fn_ref[r0:r0 + rc, sl] = (f_ref[r0:r0 + rc, sl].astype(F32) * inv * fg_ref[:, sl]).astype(BF16)
    o_ref[...] = (jnp.dot(a_ref[...], w_ref[:ATTN_WIDTH, :], preferred_element_type=F32)
                  + jnp.dot(fn_ref[...], w_ref[ATTN_WIDTH:, :], preferred_element_type=F32))
    _post_into(o_ref, x_ref, post_g_ref, mod_ref, 1.0)


def _outproj(a, f, x, mod, post_g, f_gain, w_out, *, layer, seq, tm):
    rows, d = x.shape
    tiles_per_seq = seq // tm
    return pl.pallas_call(
        _outproj_kernel,
        grid=(rows // tm,),
        in_specs=[
            pl.BlockSpec((tm, ATTN_WIDTH), lambda r: (r, 0)),
            pl.BlockSpec((tm, FOURIER_WIDTH), lambda r: (r, 0)),
            pl.BlockSpec((tm, d), lambda r: (r, 0)),
            pl.BlockSpec((1, N_MOD, d), lambda r: (r // tiles_per_seq, 0, 0)),
            pl.BlockSpec((1, d), lambda r: (0, 0)),
            pl.BlockSpec((1, FOURIER_WIDTH), lambda r: (0, 0)),
            pl.BlockSpec((None, ATTN_WIDTH + FOURIER_WIDTH, d), lambda r: (layer, 0, 0)),
        ],
        out_specs=pl.BlockSpec((tm, d), lambda r: (r, 0)),
        out_shape=jax.ShapeDtypeStruct((rows, d), F32),
        scratch_shapes=[pltpu.VMEM((tm, FOURIER_WIDTH), BF16)],
        compiler_params=pltpu.CompilerParams(
            dimension_semantics=("parallel",), vmem_limit_bytes=V7X_VMEM_LIMIT_BYTES),
        name="outproj",
    )(a, f, x, mod, post_g.reshape(1, d), f_gain.reshape(1, FOURIER_WIDTH), w_out)


def _rope_tables(seq):
    inv_freq = ROPE_THETA ** (-jnp.arange(0, HEAD_DIM, 2, dtype=F32) / HEAD_DIM)
    ang = jnp.arange(seq, dtype=F32)[:, None] * inv_freq[None, :]
    cos, sin = jnp.cos(ang), jnp.sin(ang)
    return jnp.concatenate([cos, cos], axis=-1), jnp.concatenate([-sin, sin], axis=-1)


def _dft_tables(seq):
    n_outer = seq // BLOCK

    def cs(n, scale):
        idx = np.arange(n)
        ang = 2.0 * np.pi * ((idx[:, None] * idx[None, :]) % n) / n
        return np.cos(ang) * scale, np.sin(ang) * scale

    cc, sc = cs(FOURIER_GROUP_DIM, FOURIER_GROUP_DIM ** -0.5)
    wc = np.concatenate([cc, sc], axis=1)
    co, so = cs(n_outer, n_outer ** -0.5)
    a_mat = np.block([[co, -so], [-so, -co]])
    c128, s128 = cs(BLOCK, BLOCK ** -0.5)
    n2 = np.arange(BLOCK)[:, None]
    k1 = np.arange(n_outer)[None, :]
    tw = 2.0 * np.pi * ((n2 * k1) % seq) / seq
    tw_cos = np.broadcast_to(np.cos(tw)[:, :, None], (BLOCK, n_outer, 128))
    tw_sin = np.broadcast_to(np.sin(tw)[:, :, None], (BLOCK, n_outer, 128))
    as_f32 = lambda a: jnp.asarray(np.ascontiguousarray(a), dtype=F32)
    return (as_f32(wc).astype(BF16), as_f32(a_mat).astype(BF16), as_f32(tw_cos), as_f32(tw_sin),
            as_f32(c128).astype(BF16), as_f32(s128).astype(BF16))


def _tile(seq, want):
    return min(seq, want)


def _trunk(x_groups, c_groups, w_mod, b_mod, pre_g, post_g, ffn_w_gate, ffn_w_up, ffn_w_down,
           w_in, attn_sink, fourier_w, branch_g, w_out):
    seq, d = x_groups[0].shape[1:]
    sizes = [x.shape[0] for x in x_groups]
    batch = sum(sizes)
    depth = w_mod.shape[0]
    assert seq % BLOCK == 0 and batch <= MOD_ROWS and all(x.shape[1:] == (seq, d) for x in x_groups)
    n_outer = seq // BLOCK

    c_pad = jnp.zeros((MOD_ROWS, d), F32).at[:batch].set(jnp.concatenate(c_groups, axis=0))
    mod = _modulation(c_pad, w_mod, b_mod).reshape(depth, MOD_ROWS, N_SUBLAYERS, N_MOD, d)

    cos_t, sin_t = _rope_tables(seq)
    wc, a_mat, tw_cos, tw_sin, c128, s128 = _dft_tables(seq)

    wg = ffn_w_gate.astype(BF16)
    wu = ffn_w_up.astype(BF16)
    wd = ffn_w_down.astype(BF16)
    w_in_b = w_in.astype(BF16)
    w_out_b = w_out.astype(BF16)
    w_lin = fourier_w.astype(BF16)

    tm_ffn = _tile(seq, 1024)
    tf = 512 if wg.shape[-1] % 512 == 0 else wg.shape[-1]
    tm_proj = _tile(seq, 512)
    q_blocks = min(4, n_outer)
    tiles_per_seq = seq // tm_ffn
    group_tile0 = [sum(sizes[:i]) * tiles_per_seq for i in range(len(sizes))]

    def ffn(x, l, sub, which, **kw):
        return _ffn(x, mod[l, :, sub], pre_g[l, sub], post_g[l, sub], wg, wu, wd, layer=l, which=which,
                    seq=seq, weight=0.5, tm=tm_ffn, tf=tf, **kw)

    xs = None
    for l in range(depth):
        if l == 0:
            for gi in reversed(range(len(sizes))):
                xs = ffn(x_groups[gi].reshape(sizes[gi] * seq, d), l, 0, 0,
                         tiles=sizes[gi] * tiles_per_seq, out_tile0=group_tile0[gi], seq_tile0=group_tile0[gi],
                         out_rows=batch * seq, carry=xs)
        else:
            xs = ffn(xs, l, 0, 0)
        q, k, v, pq = _inproj(xs, mod[l, :, 1], pre_g[l, 1], w_in_b, cos_t, sin_t, wc,
                              layer=l, seq=seq, tm=tm_proj)
        a_out = _attention(q, k, v, attn_sink[l], branch_g[l, 0], seq=seq, q_blocks=q_blocks)
        f_raw = _fourier_mix(pq, a_mat, tw_cos, tw_sin, c128, s128, w_lin, layer=l, batch=batch, seq=seq)
        xs = _outproj(a_out, f_raw, xs, mod[l, :, 1], post_g[l, 1], branch_g[l, 1], w_out_b,
                      layer=l, seq=seq, tm=tm_proj)
        if l < depth - 1:
            xs = ffn(xs, l, 2, 1)
    outs = []
    for gi, n in enumerate(sizes):
        y = ffn(xs, depth - 1, 2, 1, tiles=n * tiles_per_seq, in_tile0=group_tile0[gi],
                seq_tile0=group_tile0[gi], out_rows=n * seq)
        outs.append(y.reshape(n, seq, d))
    return outs


def kernel(x_prompt, x_sample, c_prompt, c_sample, w_mod, b_mod, pre_g, post_g, ffn_w_gate, ffn_w_up,
           ffn_w_down, w_in, attn_sink, fourier_w, branch_g, w_out):
    y_prompt, y_sample = _trunk([x_prompt, x_sample], [c_prompt, c_sample], w_mod, b_mod, pre_g, post_g,
                                ffn_w_gate, ffn_w_up, ffn_w_down, w_in, attn_sink, fourier_w, branch_g, w_out)
    return y_prompt, y_sample
```

```python
import functools

import numpy as np
import jax
import jax.numpy as jnp
from jax import lax
from jax.experimental import pallas as pl
from jax.experimental.pallas import tpu as pltpu

HEAD_DIM = 128
N_Q_HEADS = 8
N_KV_HEADS = 2
Q_PER_KV = N_Q_HEADS // N_KV_HEADS
ATTN_WIDTH = N_Q_HEADS * HEAD_DIM
KV_WIDTH = N_KV_HEADS * HEAD_DIM
N_FOURIER_GROUPS = 8
FOURIER_GROUP_DIM = 128
FOURIER_WIDTH = N_FOURIER_GROUPS * FOURIER_GROUP_DIM
WINDOW = 128
BLOCK = 128
ROPE_THETA = 10000.0
N_SUBLAYERS = 3
N_MOD = 3
RMS_EPS = 1e-6
NEG_INF = -1e30
LOG2_E = 1.4426950408889634
MOD_ROWS = 8

V7X_VMEM_LIMIT_BYTES = 60 * 1024 * 1024

BF16 = jnp.bfloat16
F32 = jnp.float32


def _sigmoid(x):
    return 1.0 / (1.0 + jnp.exp(-x))


LANES = 128
ROW_CHUNK = 128


def _row_rms_scale(ref, r0, rows):
    d = ref.shape[-1]
    acc = None
    for c0 in range(0, d, LANES):
        t = ref[r0:r0 + rows, c0:c0 + LANES]
        acc = t * t if acc is None else acc + t * t
    return lax.rsqrt(jnp.sum(acc, axis=-1, keepdims=True) * (1.0 / d) + RMS_EPS)


def _pre_into(h_ref, x_ref, gain_ref, mod_ref, mult_ref):
    rows, d = x_ref.shape
    rc = min(ROW_CHUNK, rows)
    mult_ref[...] = gain_ref[...] * (1.0 + mod_ref[0, 1:2, :])
    for r0 in range(0, rows, rc):
        inv = _row_rms_scale(x_ref, r0, rc)
        for c0 in range(0, d, LANES):
            sl = slice(c0, c0 + LANES)
            t = (x_ref[r0:r0 + rc, sl] * inv) * mult_ref[:, sl] + mod_ref[0, 0:1, sl]
            h_ref[r0:r0 + rc, sl] = t.astype(h_ref.dtype)


def _post_into(o_ref, x_ref, gain_ref, mod_ref, mult_ref, weight):
    rows, d = x_ref.shape
    rc = min(ROW_CHUNK, rows)
    mult_ref[...] = (weight * (1.0 + mod_ref[0, 2:3, :])) * gain_ref[...]
    for r0 in range(0, rows, rc):
        inv = _row_rms_scale(o_ref, r0, rc)
        for c0 in range(0, d, LANES):
            sl = slice(c0, c0 + LANES)
            y = o_ref[r0:r0 + rc, sl]
            o_ref[r0:r0 + rc, sl] = x_ref[r0:r0 + rc, sl] + (y * inv) * mult_ref[:, sl]


def _mod_kernel(c_ref, w_ref, b_ref, o_ref):
    c = c_ref[...]
    act = (c * _sigmoid(c)).astype(BF16)
    o_ref[0] = jnp.dot(act, w_ref[0].astype(BF16), preferred_element_type=F32) + b_ref[0]


def _modulation(c_pad, w_mod, b_mod):
    depth, d, width = w_mod.shape
    tn = 1024
    return pl.pallas_call(
        _mod_kernel,
        grid=(depth, width // tn),
        in_specs=[
            pl.BlockSpec((MOD_ROWS, d), lambda l, n: (0, 0)),
            pl.BlockSpec((1, d, tn), lambda l, n: (l, 0, n)),
            pl.BlockSpec((1, 1, tn), lambda l, n: (l, 0, n)),
        ],
        out_specs=pl.BlockSpec((1, MOD_ROWS, tn), lambda l, n: (l, 0, n)),
        out_shape=jax.ShapeDtypeStruct((depth, MOD_ROWS, width), F32),
        compiler_params=pltpu.CompilerParams(dimension_semantics=("parallel", "parallel")),
        name="modulation",
    )(c_pad, w_mod, b_mod.reshape(depth, 1, width))


def _ffn_kernel(x_ref, mod_ref, pre_g_ref, post_g_ref, wg_ref, wu_ref, wd_ref, o_ref, h_ref, mult_ref,
                *, n_chunks, weight, active_tiles):
    j = pl.program_id(1)

    def chunk_step():
        @pl.when(j == 0)
        def _():
            _pre_into(h_ref, x_ref, pre_g_ref, mod_ref, mult_ref)
            o_ref[...] = jnp.zeros_like(o_ref)

        h = h_ref[...]
        tf = wg_ref.shape[1]
        halves = [slice(c0, c0 + tf // 2) for c0 in (0, tf // 2)]
        gu = [(jnp.dot(h, wg_ref[:, sl], preferred_element_type=F32),
               jnp.dot(h, wu_ref[:, sl], preferred_element_type=F32)) for sl in halves]
        down = None
        for sl, (g, u) in zip(halves, gu):
            a = ((g * _sigmoid(g)) * u).astype(BF16)
            part = jnp.dot(a, wd_ref[sl, :], preferred_element_type=F32)
            down = part if down is None else down + part
        o_ref[...] += down

        @pl.when(j == n_chunks - 1)
        def _():
            _post_into(o_ref, x_ref, post_g_ref, mod_ref, mult_ref, weight)

    if active_tiles is None:
        chunk_step()
    else:
        lo, hi = active_tiles
        r = pl.program_id(0)
        active = (r >= lo) & (r < hi)
        pl.when(active)(chunk_step)

        @pl.when(jnp.logical_not(active) & (j == 0))
        def _():
            o_ref[...] = jnp.zeros_like(o_ref)


def _ffn_carry_kernel(x_ref, mod_ref, pre_g_ref, post_g_ref, wg_ref, wu_ref, wd_ref, carry_ref, o_ref, h_ref,
                      mult_ref, **kw):
    del carry_ref
    _ffn_kernel(x_ref, mod_ref, pre_g_ref, post_g_ref, wg_ref, wu_ref, wd_ref, o_ref, h_ref, mult_ref, **kw)


def _ffn(x, mod, pre_g, post_g, wg, wu, wd, *, layer, which, seq, weight, tm, tf,
         tiles=None, in_tile0=0, out_tile0=0, seq_tile0=0, out_rows=None, carry=None):
    d = x.shape[1]
    d_ff = wg.shape[-1]
    tiles = x.shape[0] // tm if tiles is None else tiles
    out_rows = x.shape[0] if out_rows is None else out_rows
    tiles_per_seq = seq // tm
    n_chunks = d_ff // tf
    if carry is None:
        grid_tiles, first = out_rows // tm, out_tile0
    else:
        grid_tiles, first = tiles, 0
    active = None if grid_tiles == tiles else (first, first + tiles)

    def local(r):
        return jnp.clip(r - first, 0, tiles - 1)

    def chunk(r, j):
        return j if active is None else jnp.where((r >= first) & (r < first + tiles), j, 0)

    in_specs = [
        pl.BlockSpec((tm, d), lambda r, j: (local(r) + in_tile0, 0)),
        pl.BlockSpec((1, N_MOD, d), lambda r, j: ((local(r) + seq_tile0) // tiles_per_seq, 0, 0)),
        pl.BlockSpec((1, d), lambda r, j: (0, 0)),
        pl.BlockSpec((1, d), lambda r, j: (0, 0)),
        pl.BlockSpec((None, None, d, tf), lambda r, j: (layer, which, 0, chunk(r, j))),
        pl.BlockSpec((None, None, d, tf), lambda r, j: (layer, which, 0, chunk(r, j))),
        pl.BlockSpec((None, None, tf, d), lambda r, j: (layer, which, chunk(r, j), 0)),
    ]
    args = [x, mod, pre_g.reshape(1, d), post_g.reshape(1, d), wg, wu, wd]
    body, aliases = _ffn_kernel, {}
    if carry is not None:
        in_specs.append(pl.BlockSpec(memory_space=pl.ANY))
        args.append(carry)
        body, aliases = _ffn_carry_kernel, {len(args) - 1: 0}
    return pl.pallas_call(
        functools.partial(body, n_chunks=n_chunks, weight=weight, active_tiles=active),
        grid=(grid_tiles, n_chunks),
        in_specs=in_specs,
        out_specs=pl.BlockSpec((tm, d), lambda r, j: (r - first + out_tile0, 0)),
        out_shape=jax.ShapeDtypeStruct((out_rows, d), F32),
        scratch_shapes=[pltpu.VMEM((tm, d), BF16), pltpu.VMEM((1, d), F32)],
        input_output_aliases=aliases,
        compiler_params=pltpu.CompilerParams(
            dimension_semantics=("parallel", "arbitrary"),
            vmem_limit_bytes=V7X_VMEM_LIMIT_BYTES),
        name="ffn",
    )(*args)


def _inproj_kernel(x_ref, mod_ref, pre_g_ref, w_ref, cos_ref, sin_ref, wc_ref,
                   q_ref, k_ref, v_ref, pq_ref, h_ref, mult_ref):
    _pre_into(h_ref, x_ref, pre_g_ref, mod_ref, mult_ref)
    proj = jnp.dot(h_ref[...], w_ref[...], preferred_element_type=F32)
    cos_t = cos_ref[...]
    sin_t = sin_ref[...]

    def rope(t):
        return t * cos_t + pltpu.roll(t, HEAD_DIM // 2, 1) * sin_t

    q_scale = LOG2_E * HEAD_DIM ** -0.5
    for hh in range(N_Q_HEADS):
        sl = slice(hh * HEAD_DIM, (hh + 1) * HEAD_DIM)
        q_ref[:, sl] = (rope(proj[:, sl]) * q_scale).astype(BF16)
    for hh in range(N_KV_HEADS):
        src = slice(ATTN_WIDTH + hh * HEAD_DIM, ATTN_WIDTH + (hh + 1) * HEAD_DIM)
        k_ref[:, hh * HEAD_DIM:(hh + 1) * HEAD_DIM] = rope(proj[:, src]).astype(BF16)
    v_ref[...] = proj[:, ATTN_WIDTH + KV_WIDTH:ATTN_WIDTH + 2 * KV_WIDTH].astype(BF16)
    u0 = ATTN_WIDTH + 2 * KV_WIDTH
    wc = wc_ref[...]
    for g in range(N_FOURIER_GROUPS):
        dst = slice(g * FOURIER_GROUP_DIM, (g + 1) * FOURIER_GROUP_DIM)
        ug = proj[:, u0 + g * FOURIER_GROUP_DIM:u0 + (g + 1) * FOURIER_GROUP_DIM].astype(BF16)
        pq = jnp.dot(ug, wc, preferred_element_type=F32)
        pq_ref[0, :, dst] = pq[:, :FOURIER_GROUP_DIM].astype(BF16)
        pq_ref[1, :, dst] = pq[:, FOURIER_GROUP_DIM:].astype(BF16)


def _inproj(x, mod, pre_g, w_in, cos_t, sin_t, wc, *, layer, seq, tm):
    rows, d = x.shape
    tiles_per_seq = seq // tm
    width = w_in.shape[-1]
    return pl.pallas_call(
        _inproj_kernel,
        grid=(rows // tm,),
        in_specs=[
            pl.BlockSpec((tm, d), lambda r: (r, 0)),
            pl.BlockSpec((1, N_MOD, d), lambda r: (r // tiles_per_seq, 0, 0)),
            pl.BlockSpec((1, d), lambda r: (0, 0)),
            pl.BlockSpec((None, d, width), lambda r: (layer, 0, 0)),
            pl.BlockSpec((tm, HEAD_DIM), lambda r: (r % tiles_per_seq, 0)),
            pl.BlockSpec((tm, HEAD_DIM), lambda r: (r % tiles_per_seq, 0)),
            pl.BlockSpec((FOURIER_GROUP_DIM, 2 * FOURIER_GROUP_DIM), lambda r: (0, 0)),
        ],
        out_specs=[
            pl.BlockSpec((tm, ATTN_WIDTH), lambda r: (r, 0)),
            pl.BlockSpec((tm, KV_WIDTH), lambda r: (r, 0)),
            pl.BlockSpec((tm, KV_WIDTH), lambda r: (r, 0)),
            pl.BlockSpec((2, tm, FOURIER_WIDTH), lambda r: (0, r, 0)),
        ],
        out_shape=[
            jax.ShapeDtypeStruct((rows, ATTN_WIDTH), BF16),
            jax.ShapeDtypeStruct((rows, KV_WIDTH), BF16),
            jax.ShapeDtypeStruct((rows, KV_WIDTH), BF16),
            jax.ShapeDtypeStruct((2, rows, FOURIER_WIDTH), BF16),
        ],
        scratch_shapes=[pltpu.VMEM((tm, d), BF16), pltpu.VMEM((1, d), F32)],
        compiler_params=pltpu.CompilerParams(
            dimension_semantics=("parallel",), vmem_limit_bytes=V7X_VMEM_LIMIT_BYTES),
        name="inproj",
    )(x, mod, pre_g.reshape(1, d), w_in, cos_t, sin_t, wc)


def _attn_kernel(sink_ref, q_ref, kp_ref, kc_ref, kn_ref, vp_ref, vc_ref, vn_ref, g_ref,
                 o_ref, kbuf, vbuf, *, q_blocks, blocks_per_seq):
    tq = q_blocks * BLOCK
    band = 3 * BLOCK
    ext = 2 * HEAD_DIM
    kbuf[0:BLOCK] = kp_ref[...]
    kbuf[BLOCK:BLOCK + tq] = kc_ref[...]
    kbuf[BLOCK + tq:2 * BLOCK + tq] = kn_ref[...]
    for hk in range(N_KV_HEADS):
        src = slice(hk * HEAD_DIM, (hk + 1) * HEAD_DIM)
        dst = slice(hk * ext, hk * ext + HEAD_DIM)
        vbuf[0:BLOCK, dst] = vp_ref[:, src]
        vbuf[BLOCK:BLOCK + tq, dst] = vc_ref[:, src]
        vbuf[BLOCK + tq:2 * BLOCK + tq, dst] = vn_ref[:, src]
        vbuf[:, hk * ext + HEAD_DIM:(hk + 1) * ext] = jnp.ones((tq + 2 * BLOCK, HEAD_DIM), BF16)

    first_block = (pl.program_id(0) * q_blocks) % blocks_per_seq
    qi = lax.broadcasted_iota(jnp.int32, (BLOCK, BLOCK), 0)
    kj = lax.broadcasted_iota(jnp.int32, (BLOCK, BLOCK), 1)
    tri_prev = jnp.where(kj >= qi, 0.0, NEG_INF)
    tri_next = jnp.where(kj <= qi, 0.0, NEG_INF)
    gain = g_ref[...]
    n_chains = q_blocks * N_KV_HEADS

    def scores(c):
        b, hk = divmod(c, N_KV_HEADS)
        r0 = b * BLOCK
        qs = jnp.concatenate(
            [q_ref[r0:r0 + BLOCK, (hk * Q_PER_KV + g) * HEAD_DIM:(hk * Q_PER_KV + g + 1) * HEAD_DIM]
             for g in range(Q_PER_KV)], axis=0)
        kb = kbuf[r0:r0 + band, hk * HEAD_DIM:(hk + 1) * HEAD_DIM]
        return lax.dot_general(qs, kb, (((1,), (1,)), ((), ())), preferred_element_type=F32)

    def softmax(c, s):
        b, hk = divmod(c, N_KV_HEADS)
        n = first_block + b
        bias_prev = tri_prev + jnp.where(n == 0, NEG_INF, 0.0)
        bias_next = tri_next + jnp.where(n == blocks_per_seq - 1, NEG_INF, 0.0)
        ps, sink_terms = [], []
        for g in range(Q_PER_KV):
            sg = s[g * BLOCK:(g + 1) * BLOCK]
            s0 = sg[:, 0:BLOCK] + bias_prev
            s1 = sg[:, BLOCK:2 * BLOCK]
            s2 = sg[:, 2 * BLOCK:] + bias_next
            sink = sink_ref[hk * Q_PER_KV + g] * LOG2_E
            m = jnp.maximum(jnp.max(jnp.maximum(jnp.maximum(s0, s1), s2), axis=-1, keepdims=True), sink)
            ps.append(jnp.concatenate([jnp.exp2(s0 - m), jnp.exp2(s1 - m), jnp.exp2(s2 - m)],
                                      axis=1).astype(BF16))
            sink_terms.append(jnp.exp2(sink - m))
        return jnp.concatenate(ps, axis=0), sink_terms

    def weighted_values(c, p, sink_terms):
        b, hk = divmod(c, N_KV_HEADS)
        r0 = b * BLOCK
        oe = jnp.dot(p, vbuf[r0:r0 + band, hk * ext:(hk + 1) * ext], preferred_element_type=F32)
        outs = []
        for g in range(Q_PER_KV):
            og = oe[g * BLOCK:(g + 1) * BLOCK]
            denom = og[:, HEAD_DIM:] + sink_terms[g]
            outs.append(og[:, :HEAD_DIM] * (1.0 / denom))
        return outs

    def finish_block(b, heads):
        r0 = b * BLOCK
        sq = heads[0] * heads[0]
        for t in heads[1:]:
            sq = sq + t * t
        inv = lax.rsqrt(jnp.sum(sq, axis=-1, keepdims=True) * (1.0 / ATTN_WIDTH) + RMS_EPS)
        for hh, t in enumerate(heads):
            sl = slice(hh * HEAD_DIM, (hh + 1) * HEAD_DIM)
            o_ref[r0:r0 + BLOCK, sl] = (t * inv * gain[:, sl]).astype(BF16)

    lead = 2
    s_vals = {c: scores(c) for c in range(min(lead, n_chains))}
    p_vals, heads = {}, {}
    for c in range(n_chains):
        if c + lead < n_chains:
            s_vals[c + lead] = scores(c + lead)
        p_vals[c] = softmax(c, s_vals.pop(c))
        for done in ([c - 1] if c >= 1 else []) + ([c] if c == n_chains - 1 else []):
            b, hk = divmod(done, N_KV_HEADS)
            heads.setdefault(b, []).extend(weighted_values(done, *p_vals.pop(done)))
            if hk == N_KV_HEADS - 1:
                finish_block(b, heads.pop(b))


def _attention(q, k, v, sink, gain, *, seq, q_blocks):
    rows = q.shape[0]
    tq = q_blocks * BLOCK
    bps = seq // BLOCK

    def prev_map(r):
        g0 = r * q_blocks
        return (jnp.where(g0 % bps == 0, g0, g0 - 1), 0)

    def next_map(r):
        g1 = (r + 1) * q_blocks
        return (jnp.where(g1 % bps == 0, g1 - 1, g1), 0)

    edge = pl.BlockSpec((BLOCK, KV_WIDTH), prev_map)
    edge_n = pl.BlockSpec((BLOCK, KV_WIDTH), next_map)
    cur = pl.BlockSpec((tq, KV_WIDTH), lambda r: (r, 0))
    return pl.pallas_call(
        functools.partial(_attn_kernel, q_blocks=q_blocks, blocks_per_seq=bps),
        grid=(rows // tq,),
        in_specs=[
            pl.BlockSpec(memory_space=pltpu.SMEM),
            pl.BlockSpec((tq, ATTN_WIDTH), lambda r: (r, 0)),
            edge, cur, edge_n, edge, cur, edge_n,
            pl.BlockSpec((1, ATTN_WIDTH), lambda r: (0, 0)),
        ],
        out_specs=pl.BlockSpec((tq, ATTN_WIDTH), lambda r: (r, 0)),
        out_shape=jax.ShapeDtypeStruct((rows, ATTN_WIDTH), BF16),
        scratch_shapes=[pltpu.VMEM((tq + 2 * BLOCK, KV_WIDTH), BF16),
                        pltpu.VMEM((tq + 2 * BLOCK, 2 * KV_WIDTH), BF16)],
        compiler_params=pltpu.CompilerParams(dimension_semantics=("parallel",)),
        name="attention",
    )(sink, q, k, k, k, v, v, v, gain.reshape(1, ATTN_WIDTH))


FFT_COLS = 8


SUBLANES = 8


def _pitch(rows):
    return rows + SUBLANES if (rows // SUBLANES) % 2 == 0 else rows


def _fft_kernel(pq_ref, a_ref, tc_ref, ts_ref, c_ref, s_ref, wl_ref, o_ref, p_s, q_s, z_s, *, n_outer):
    in_pitch = _pitch(BLOCK)
    out_pitch = _pitch(n_outer)
    for n1 in range(n_outer):
        rows = slice(n1 * BLOCK, (n1 + 1) * BLOCK)
        p_s[n1 * in_pitch:n1 * in_pitch + BLOCK, :] = pq_ref[0, rows, :].astype(F32)
        q_s[n1 * in_pitch:n1 * in_pitch + BLOCK, :] = pq_ref[1, rows, :].astype(F32)

    def slow_rows(n2):
        return pl.ds(n2, n_outer, stride=in_pitch)

    a_mat = a_ref[...]
    for n2_0 in range(0, BLOCK, FFT_COLS):
        cols = range(n2_0, n2_0 + FFT_COLS)
        x = jnp.concatenate([jnp.concatenate([p_s[slow_rows(n2), :] for n2 in cols], axis=1),
                             jnp.concatenate([q_s[slow_rows(n2), :] for n2 in cols], axis=1)], axis=0)
        y = jnp.dot(a_mat, x.astype(BF16), preferred_element_type=F32)
        for t, n2 in enumerate(cols):
            yr = y[:n_outer, t * LANES:(t + 1) * LANES]
            yi = y[n_outer:, t * LANES:(t + 1) * LANES]
            tc = tc_ref[n2]
            ts = ts_ref[n2]
            p_s[slow_rows(n2), :] = yr * tc + yi * ts
            q_s[slow_rows(n2), :] = yi * tc - yr * ts

    c_mat = c_ref[...]
    s_mat = s_ref[...]
    w_lin = wl_ref[...]
    step = min(FFT_COLS, n_outer)
    for k1_0 in range(0, n_outer, step):
        ks = range(k1_0, k1_0 + step)
        yr = jnp.concatenate([p_s[k1 * in_pitch:k1 * in_pitch + BLOCK, :] for k1 in ks], axis=1).astype(BF16)
        yi = jnp.concatenate([q_s[k1 * in_pitch:k1 * in_pitch + BLOCK, :] for k1 in ks], axis=1).astype(BF16)
        z = (jnp.dot(c_mat, yr, preferred_element_type=F32)
             + jnp.dot(s_mat, yi, preferred_element_type=F32))
        z_rows = jnp.concatenate([z[:, t * LANES:(t + 1) * LANES] for t in range(step)], axis=0).astype(BF16)
        out = jnp.dot(z_rows, w_lin, preferred_element_type=F32)
        for t, k1 in enumerate(ks):
            z_s[pl.ds(k1, BLOCK, stride=out_pitch), :] = out[t * BLOCK:(t + 1) * BLOCK]
    for k2 in range(BLOCK):
        o_ref[k2 * n_outer:(k2 + 1) * n_outer, :] = z_s[k2 * out_pitch:k2 * out_pitch + n_outer, :].astype(o_ref.dtype)


def _fourier_mix(pq, a_mat, tw_cos, tw_sin, c_mat, s_mat, w_lin, *, layer, batch, seq):
    n_outer = seq // BLOCK
    const = functools.partial(pl.BlockSpec, pipeline_mode=pl.Buffered(1))
    return pl.pallas_call(
        functools.partial(_fft_kernel, n_outer=n_outer),
        grid=(batch, N_FOURIER_GROUPS),
        in_specs=[
            pl.BlockSpec((2, seq, LANES), lambda b, g: (0, b, g)),
            const((2 * n_outer, 2 * n_outer), lambda b, g: (0, 0)),
            const((BLOCK, n_outer, LANES), lambda b, g: (0, 0, 0)),
            const((BLOCK, n_outer, LANES), lambda b, g: (0, 0, 0)),
            const((BLOCK, BLOCK), lambda b, g: (0, 0)),
            const((BLOCK, BLOCK), lambda b, g: (0, 0)),
            pl.BlockSpec((None, None, FOURIER_GROUP_DIM, FOURIER_GROUP_DIM), lambda b, g: (layer, g, 0, 0)),
        ],
        out_specs=pl.BlockSpec((seq, LANES), lambda b, g: (b, g)),
        out_shape=jax.ShapeDtypeStruct((batch * seq, FOURIER_WIDTH), BF16),
        scratch_shapes=[pltpu.VMEM((n_outer * _pitch(BLOCK), LANES), F32),
                        pltpu.VMEM((n_outer * _pitch(BLOCK), LANES), F32),
                        pltpu.VMEM((BLOCK * _pitch(n_outer), LANES), F32)],
        compiler_params=pltpu.CompilerParams(
            dimension_semantics=("parallel", "parallel"), vmem_limit_bytes=V7X_VMEM_LIMIT_BYTES),
        name="fourier_mix",
    )(pq, a_mat, tw_cos, tw_sin, c_mat, s_mat, w_lin)


def _outproj_kernel(a_ref, f_ref, x_ref, mod_ref, post_g_ref, fg_ref, w_ref, o_ref, fn_ref, mult_ref):
    rows = f_ref.shape[0]
    rc = min(ROW_CHUNK, rows)
    for r0 in range(0, rows, rc):
        acc = None
        for c0 in range(0, FOURIER_WIDTH, LANES):
            t = f_ref[r0:r0 + rc, c0:c0 + LANES].astype(F32)
            acc = t * t if acc is None else acc + t * t
        inv = lax.rsqrt(jnp.sum(acc, axis=-1, keepdims=True) * (1.0 / FOURIER_WIDTH) + RMS_EPS)
        for c0 in range(0, FOURIER_WIDTH, LANES):
            sl = slice(c0, c0 + LANES)
            fn_ref[r0:r0 + rc, sl] = (f_ref[r0:r0 + rc, sl].astype(F32) * inv * fg_ref[:, sl]).astype(BF16)
    o_ref[...] = (jnp.dot(a_ref[...], w_ref[:ATTN_WIDTH, :], preferred_element_type=F32)
                  + jnp.dot(fn_ref[...], w_ref[ATTN_WIDTH:, :], preferred_element_type=F32))
    _post_into(o_ref, x_ref, post_g_ref, mod_ref, mult_ref, 1.0)


def _outproj(a, f, x, mod, post_g, f_gain, w_out, *, layer, seq, tm):
    rows, d = x.shape
    tiles_per_seq = seq // tm
    return pl.pallas_call(
        _outproj_kernel,
        grid=(rows // tm,),
        in_specs=[
            pl.BlockSpec((tm, ATTN_WIDTH), lambda r: (r, 0)),
            pl.BlockSpec((tm, FOURIER_WIDTH), lambda r: (r, 0)),
            pl.BlockSpec((tm, d), lambda r: (r, 0)),
            pl.BlockSpec((1, N_MOD, d), lambda r: (r // tiles_per_seq, 0, 0)),
            pl.BlockSpec((1, d), lambda r: (0, 0)),
            pl.BlockSpec((1, FOURIER_WIDTH), lambda r: (0, 0)),
            pl.BlockSpec((None, ATTN_WIDTH + FOURIER_WIDTH, d), lambda r: (layer, 0, 0)),
        ],
        out_specs=pl.BlockSpec((tm, d), lambda r: (r, 0)),
        out_shape=jax.ShapeDtypeStruct((rows, d), F32),
        scratch_shapes=[pltpu.VMEM((tm, FOURIER_WIDTH), BF16), pltpu.VMEM((1, d), F32)],
        compiler_params=pltpu.CompilerParams(
            dimension_semantics=("parallel",), vmem_limit_bytes=V7X_VMEM_LIMIT_BYTES),
        name="outproj",
    )(a, f, x, mod, post_g.reshape(1, d), f_gain.reshape(1, FOURIER_WIDTH), w_out)


def _rope_tables(seq):
    inv_freq = ROPE_THETA ** (-jnp.arange(0, HEAD_DIM, 2, dtype=F32) / HEAD_DIM)
    ang = jnp.arange(seq, dtype=F32)[:, None] * inv_freq[None, :]
    cos, sin = jnp.cos(ang), jnp.sin(ang)
    return jnp.concatenate([cos, cos], axis=-1), jnp.concatenate([-sin, sin], axis=-1)


def _dft_tables(seq):
    n_outer = seq // BLOCK

    def cs(n, scale):
        idx = np.arange(n)
        ang = 2.0 * np.pi * ((idx[:, None] * idx[None, :]) % n) / n
        return np.cos(ang) * scale, np.sin(ang) * scale

    cc, sc = cs(FOURIER_GROUP_DIM, FOURIER_GROUP_DIM ** -0.5)
    wc = np.concatenate([cc, sc], axis=1)
    co, so = cs(n_outer, n_outer ** -0.5)
    a_mat = np.block([[co, -so], [-so, -co]])
    c128, s128 = cs(BLOCK, BLOCK ** -0.5)
    n2 = np.arange(BLOCK)[:, None]
    k1 = np.arange(n_outer)[None, :]
    tw = 2.0 * np.pi * ((n2 * k1) % seq) / seq
    tw_cos = np.broadcast_to(np.cos(tw)[:, :, None], (BLOCK, n_outer, 128))
    tw_sin = np.broadcast_to(np.sin(tw)[:, :, None], (BLOCK, n_outer, 128))
    as_f32 = lambda a: jnp.asarray(np.ascontiguousarray(a), dtype=F32)
    return (as_f32(wc).astype(BF16), as_f32(a_mat).astype(BF16), as_f32(tw_cos), as_f32(tw_sin),
            as_f32(c128).astype(BF16), as_f32(s128).astype(BF16))


def _tile(seq, want):
    return min(seq, want)


def _trunk(x_groups, c_groups, w_mod, b_mod, pre_g, post_g, ffn_w_gate, ffn_w_up, ffn_w_down,
           w_in, attn_sink, fourier_w, branch_g, w_out):
    seq, d = x_groups[0].shape[1:]
    sizes = [x.shape[0] for x in x_groups]
    batch = sum(sizes)
    depth = w_mod.shape[0]
    assert seq % BLOCK == 0 and batch <= MOD_ROWS and all(x.shape[1:] == (seq, d) for x in x_groups)
    n_outer = seq // BLOCK

    c_pad = jnp.zeros((MOD_ROWS, d), F32).at[:batch].set(jnp.concatenate(c_groups, axis=0))
    mod = _modulation(c_pad, w_mod, b_mod).reshape(depth, MOD_ROWS, N_SUBLAYERS, N_MOD, d)

    cos_t, sin_t = _rope_tables(seq)
    wc, a_mat, tw_cos, tw_sin, c128, s128 = _dft_tables(seq)

    wg = ffn_w_gate.astype(BF16)
    wu = ffn_w_up.astype(BF16)
    wd = ffn_w_down.astype(BF16)
    w_in_b = w_in.astype(BF16)
    w_out_b = w_out.astype(BF16)
    w_lin = fourier_w.astype(BF16)

    tm_ffn = _tile(seq, 1024)
    tf = 512 if wg.shape[-1] % 512 == 0 else wg.shape[-1]
    tm_proj = _tile(seq, 512)
    q_blocks = min(8, n_outer)
    tiles_per_seq = seq // tm_ffn
    group_tile0 = [sum(sizes[:i]) * tiles_per_seq for i in range(len(sizes))]

    def ffn(x, l, sub, which, **kw):
        return _ffn(x, mod[l, :, sub], pre_g[l, sub], post_g[l, sub], wg, wu, wd, layer=l, which=which,
                    seq=seq, weight=0.5, tm=tm_ffn, tf=tf, **kw)

    xs = None
    for l in range(depth):
        if l == 0:
            for gi in reversed(range(len(sizes))):
                xs = ffn(x_groups[gi].reshape(sizes[gi] * seq, d), l, 0, 0,
                         tiles=sizes[gi] * tiles_per_seq, out_tile0=group_tile0[gi], seq_tile0=group_tile0[gi],
                         out_rows=batch * seq, carry=xs)
        else:
            xs = ffn(xs, l, 0, 0)
        q, k, v, pq = _inproj(xs, mod[l, :, 1], pre_g[l, 1], w_in_b, cos_t, sin_t, wc,
                              layer=l, seq=seq, tm=tm_proj)
        a_out = _attention(q, k, v, attn_sink[l], branch_g[l, 0], seq=seq, q_blocks=q_blocks)
        f_raw = _fourier_mix(pq, a_mat, tw_cos, tw_sin, c128, s128, w_lin, layer=l, batch=batch, seq=seq)
        xs = _outproj(a_out, f_raw, xs, mod[l, :, 1], post_g[l, 1], branch_g[l, 1], w_out_b,
                      layer=l, seq=seq, tm=tm_proj)
        if l < depth - 1:
            xs = ffn(xs, l, 2, 1)
    outs = []
    for gi, n in enumerate(sizes):
        y = ffn(xs, depth - 1, 2, 1, tiles=n * tiles_per_seq, in_tile0=group_tile0[gi],
                seq_tile0=group_tile0[gi], out_rows=n * seq)
        outs.append(y.reshape(n, seq, d))
    return outs


def kernel(x_prompt, x_sample, c_prompt, c_sample, w_mod, b_mod, pre_g, post_g, ffn_w_gate, ffn_w_up,
           ffn_w_down, w_in, attn_sink, fourier_w, branch_g, w_out):
    y_prompt, y_sample = _trunk([x_prompt, x_sample], [c_prompt, c_sample], w_mod, b_mod, pre_g, post_g,
                                ffn_w_gate, ffn_w_up, ffn_w_down, w_in, attn_sink, fourier_w, branch_g, w_out)
    return y_prompt, y_sample
```

```python
import functools

import numpy as np
import jax
import jax.numpy as jnp
from jax import lax
from jax.experimental import pallas as pl
from jax.experimental.pallas import tpu as pltpu

HEAD_DIM = 128
N_Q_HEADS = 8
N_KV_HEADS = 2
Q_PER_KV = N_Q_HEADS // N_KV_HEADS
ATTN_WIDTH = N_Q_HEADS * HEAD_DIM
KV_WIDTH = N_KV_HEADS * HEAD_DIM
N_FOURIER_GROUPS = 8
FOURIER_GROUP_DIM = 128
FOURIER_WIDTH = N_FOURIER_GROUPS * FOURIER_GROUP_DIM
WINDOW = 128
BLOCK = 128
ROPE_THETA = 10000.0
N_SUBLAYERS = 3
N_MOD = 3
RMS_EPS = 1e-6
NEG_INF = -1e30
LOG2_E = 1.4426950408889634
MOD_ROWS = 8

V7X_VMEM_LIMIT_BYTES = 60 * 1024 * 1024

BF16 = jnp.bfloat16
F32 = jnp.float32


def _sigmoid(x):
    return 1.0 / (1.0 + jnp.exp(-x))


LANES = 128
ROW_CHUNK = 64


def _row_rms_scale(ref, r0, rows):
    d = ref.shape[-1]
    acc = None
    for c0 in range(0, d, LANES):
        t = ref[r0:r0 + rows, c0:c0 + LANES]
        acc = t * t if acc is None else acc + t * t
    return lax.rsqrt(jnp.sum(acc, axis=-1, keepdims=True) * (1.0 / d) + RMS_EPS)


def _pre_into(h_ref, x_ref, gain_ref, mod_ref, mult_ref):
    rows, d = x_ref.shape
    rc = min(ROW_CHUNK, rows)
    mult_ref[...] = gain_ref[...] * (1.0 + mod_ref[0, 1:2, :])
    for r0 in range(0, rows, rc):
        inv = _row_rms_scale(x_ref, r0, rc)
        for c0 in range(0, d, LANES):
            sl = slice(c0, c0 + LANES)
            t = (x_ref[r0:r0 + rc, sl] * inv) * mult_ref[:, sl] + mod_ref[0, 0:1, sl]
            h_ref[r0:r0 + rc, sl] = t.astype(h_ref.dtype)


def _post_into(o_ref, x_ref, gain_ref, mod_ref, mult_ref, weight):
    rows, d = x_ref.shape
    rc = min(ROW_CHUNK, rows)
    mult_ref[...] = (weight * (1.0 + mod_ref[0, 2:3, :])) * gain_ref[...]
    for r0 in range(0, rows, rc):
        inv = _row_rms_scale(o_ref, r0, rc)
        for c0 in range(0, d, LANES):
            sl = slice(c0, c0 + LANES)
            y = o_ref[r0:r0 + rc, sl]
            o_ref[r0:r0 + rc, sl] = x_ref[r0:r0 + rc, sl] + (y * inv) * mult_ref[:, sl]


def _part_specs(parts, tm):
    specs, ranges, t0 = [], [], 0
    for p in parts:
        n = p.shape[0] // tm
        specs.append(pl.BlockSpec((tm, p.shape[1]), lambda r, t0=t0, n=n: (jnp.clip(r - t0, 0, n - 1), 0)))
        ranges.append((t0, t0 + n))
        t0 += n
    return specs, ranges


def _with_owner(refs, ranges, fn):
    if len(refs) == 1:
        fn(refs[0])
        return
    r = pl.program_id(0)
    for ref, (lo, hi) in zip(refs, ranges):
        pl.when((r >= lo) & (r < hi))(functools.partial(fn, ref))


def _mod_kernel(c_ref, w_ref, b_ref, o_ref):
    c = c_ref[...]
    act = (c * _sigmoid(c)).astype(BF16)
    o_ref[0] = jnp.dot(act, w_ref[0].astype(BF16), preferred_element_type=F32) + b_ref[0]


def _modulation(c_pad, w_mod, b_mod):
    depth, d, width = w_mod.shape
    tn = 1024
    return pl.pallas_call(
        _mod_kernel,
        grid=(depth, width // tn),
        in_specs=[
            pl.BlockSpec((MOD_ROWS, d), lambda l, n: (0, 0)),
            pl.BlockSpec((1, d, tn), lambda l, n: (l, 0, n)),
            pl.BlockSpec((1, 1, tn), lambda l, n: (l, 0, n)),
        ],
        out_specs=pl.BlockSpec((1, MOD_ROWS, tn), lambda l, n: (l, 0, n)),
        out_shape=jax.ShapeDtypeStruct((depth, MOD_ROWS, width), F32),
        compiler_params=pltpu.CompilerParams(dimension_semantics=("parallel", "parallel")),
        name="modulation",
    )(c_pad, w_mod, b_mod.reshape(depth, 1, width))


def _ffn_kernel(x_ref, mod_ref, pre_g_ref, post_g_ref, wg_ref, wu_ref, wd_ref, o_ref, h_ref, mult_ref,
                *, n_chunks, weight):
    j = pl.program_id(1)

    @pl.when(j == 0)
    def _():
        _pre_into(h_ref, x_ref, pre_g_ref, mod_ref, mult_ref)
        o_ref[...] = jnp.zeros_like(o_ref)

    h = h_ref[...]
    tf = wg_ref.shape[1]
    halves = [slice(c0, c0 + tf // 2) for c0 in (0, tf // 2)]
    gu = [(jnp.dot(h, wg_ref[:, sl], preferred_element_type=F32),
           jnp.dot(h, wu_ref[:, sl], preferred_element_type=F32)) for sl in halves]
    down = None
    for sl, (g, u) in zip(halves, gu):
        a = ((g * _sigmoid(g)) * u).astype(BF16)
        part = jnp.dot(a, wd_ref[sl, :], preferred_element_type=F32)
        down = part if down is None else down + part
    o_ref[...] += down

    @pl.when(j == n_chunks - 1)
    def _():
        _post_into(o_ref, x_ref, post_g_ref, mod_ref, mult_ref, weight)


def _ffn(x, mod, pre_g, post_g, wg, wu, wd, *, layer, which, seq, weight, tm, tf,
         tiles=None, in_tile0=0, seq_tile0=0):
    d = x.shape[1]
    d_ff = wg.shape[-1]
    tiles = x.shape[0] // tm if tiles is None else tiles
    tiles_per_seq = seq // tm
    n_chunks = d_ff // tf
    return pl.pallas_call(
        functools.partial(_ffn_kernel, n_chunks=n_chunks, weight=weight),
        grid=(tiles, n_chunks),
        in_specs=[
            pl.BlockSpec((tm, d), lambda r, j: (r + in_tile0, 0)),
            pl.BlockSpec((1, N_MOD, d), lambda r, j: ((r + seq_tile0) // tiles_per_seq, 0, 0)),
            pl.BlockSpec((1, d), lambda r, j: (0, 0)),
            pl.BlockSpec((1, d), lambda r, j: (0, 0)),
            pl.BlockSpec((None, None, d, tf), lambda r, j: (layer, which, 0, j)),
            pl.BlockSpec((None, None, d, tf), lambda r, j: (layer, which, 0, j)),
            pl.BlockSpec((None, None, tf, d), lambda r, j: (layer, which, j, 0)),
        ],
        out_specs=pl.BlockSpec((tm, d), lambda r, j: (r, 0)),
        out_shape=jax.ShapeDtypeStruct((tiles * tm, d), F32),
        scratch_shapes=[pltpu.VMEM((tm, d), BF16), pltpu.VMEM((1, d), F32)],
        compiler_params=pltpu.CompilerParams(
            dimension_semantics=("parallel", "arbitrary"),
            vmem_limit_bytes=V7X_VMEM_LIMIT_BYTES),
        name="ffn",
    )(x, mod, pre_g.reshape(1, d), post_g.reshape(1, d), wg, wu, wd)


def _inproj_kernel(*refs, x_ranges):
    x_refs = refs[:len(x_ranges)]
    (mod_ref, pre_g_ref, w_ref, cos_ref, sin_ref, wc_ref,
     q_ref, k_ref, v_ref, pq_ref, h_ref, mult_ref) = refs[len(x_ranges):]
    _with_owner(x_refs, x_ranges, lambda x_ref: _pre_into(h_ref, x_ref, pre_g_ref, mod_ref, mult_ref))
    proj = jnp.dot(h_ref[...], w_ref[...], preferred_element_type=F32)
    cos_t = cos_ref[...]
    sin_t = sin_ref[...]

    def rope(t):
        return t * cos_t + pltpu.roll(t, HEAD_DIM // 2, 1) * sin_t

    q_scale = LOG2_E * HEAD_DIM ** -0.5
    for hh in range(N_Q_HEADS):
        sl = slice(hh * HEAD_DIM, (hh + 1) * HEAD_DIM)
        q_ref[:, sl] = (rope(proj[:, sl]) * q_scale).astype(BF16)
    for hh in range(N_KV_HEADS):
        src = slice(ATTN_WIDTH + hh * HEAD_DIM, ATTN_WIDTH + (hh + 1) * HEAD_DIM)
        k_ref[:, hh * HEAD_DIM:(hh + 1) * HEAD_DIM] = rope(proj[:, src]).astype(BF16)
    v_ref[...] = proj[:, ATTN_WIDTH + KV_WIDTH:ATTN_WIDTH + 2 * KV_WIDTH].astype(BF16)
    u0 = ATTN_WIDTH + 2 * KV_WIDTH
    wc = wc_ref[...]
    for g in range(N_FOURIER_GROUPS):
        dst = slice(g * FOURIER_GROUP_DIM, (g + 1) * FOURIER_GROUP_DIM)
        ug = proj[:, u0 + g * FOURIER_GROUP_DIM:u0 + (g + 1) * FOURIER_GROUP_DIM].astype(BF16)
        pq = jnp.dot(ug, wc, preferred_element_type=F32)
        pq_ref[0, :, dst] = pq[:, :FOURIER_GROUP_DIM].astype(BF16)
        pq_ref[1, :, dst] = pq[:, FOURIER_GROUP_DIM:].astype(BF16)


def _inproj(x_parts, mod, pre_g, w_in, cos_t, sin_t, wc, *, layer, seq, tm):
    rows, d = sum(p.shape[0] for p in x_parts), x_parts[0].shape[1]
    tiles_per_seq = seq // tm
    width = w_in.shape[-1]
    x_specs, x_ranges = _part_specs(x_parts, tm)
    return pl.pallas_call(
        functools.partial(_inproj_kernel, x_ranges=x_ranges),
        grid=(rows // tm,),
        in_specs=x_specs + [
            pl.BlockSpec((1, N_MOD, d), lambda r: (r // tiles_per_seq, 0, 0)),
            pl.BlockSpec((1, d), lambda r: (0, 0)),
            pl.BlockSpec((None, d, width), lambda r: (layer, 0, 0)),
            pl.BlockSpec((tm, HEAD_DIM), lambda r: (r % tiles_per_seq, 0)),
            pl.BlockSpec((tm, HEAD_DIM), lambda r: (r % tiles_per_seq, 0)),
            pl.BlockSpec((FOURIER_GROUP_DIM, 2 * FOURIER_GROUP_DIM), lambda r: (0, 0)),
        ],
        out_specs=[
            pl.BlockSpec((tm, ATTN_WIDTH), lambda r: (r, 0)),
            pl.BlockSpec((tm, KV_WIDTH), lambda r: (r, 0)),
            pl.BlockSpec((tm, KV_WIDTH), lambda r: (r, 0)),
            pl.BlockSpec((2, tm, FOURIER_WIDTH), lambda r: (0, r, 0)),
        ],
        out_shape=[
            jax.ShapeDtypeStruct((rows, ATTN_WIDTH), BF16),
            jax.ShapeDtypeStruct((rows, KV_WIDTH), BF16),
            jax.ShapeDtypeStruct((rows, KV_WIDTH), BF16),
            jax.ShapeDtypeStruct((2, rows, FOURIER_WIDTH), BF16),
        ],
        scratch_shapes=[pltpu.VMEM((tm, d), BF16), pltpu.VMEM((1, d), F32)],
        compiler_params=pltpu.CompilerParams(
            dimension_semantics=("parallel",), vmem_limit_bytes=V7X_VMEM_LIMIT_BYTES),
        name="inproj",
    )(*x_parts, mod, pre_g.reshape(1, d), w_in, cos_t, sin_t, wc)


def _attn_kernel(sink_ref, q_ref, kp_ref, kc_ref, kn_ref, vp_ref, vc_ref, vn_ref, g_ref,
                 o_ref, kbuf, vbuf, *, q_blocks, blocks_per_seq):
    tq = q_blocks * BLOCK
    band = 3 * BLOCK
    ext = 2 * HEAD_DIM
    kbuf[0:BLOCK] = kp_ref[...]
    kbuf[BLOCK:BLOCK + tq] = kc_ref[...]
    kbuf[BLOCK + tq:2 * BLOCK + tq] = kn_ref[...]
    for hk in range(N_KV_HEADS):
        src = slice(hk * HEAD_DIM, (hk + 1) * HEAD_DIM)
        dst = slice(hk * ext, hk * ext + HEAD_DIM)
        vbuf[0:BLOCK, dst] = vp_ref[:, src]
        vbuf[BLOCK:BLOCK + tq, dst] = vc_ref[:, src]
        vbuf[BLOCK + tq:2 * BLOCK + tq, dst] = vn_ref[:, src]
        vbuf[:, hk * ext + HEAD_DIM:(hk + 1) * ext] = jnp.ones((tq + 2 * BLOCK, HEAD_DIM), BF16)

    first_block = (pl.program_id(0) * q_blocks) % blocks_per_seq
    qi = lax.broadcasted_iota(jnp.int32, (BLOCK, BLOCK), 0)
    kj = lax.broadcasted_iota(jnp.int32, (BLOCK, BLOCK), 1)
    tri_prev = jnp.where(kj >= qi, 0.0, NEG_INF)
    tri_next = jnp.where(kj <= qi, 0.0, NEG_INF)
    gain = g_ref[...]
    n_chains = q_blocks * N_KV_HEADS

    def scores(c):
        b, hk = divmod(c, N_KV_HEADS)
        r0 = b * BLOCK
        qs = jnp.concatenate(
            [q_ref[r0:r0 + BLOCK, (hk * Q_PER_KV + g) * HEAD_DIM:(hk * Q_PER_KV + g + 1) * HEAD_DIM]
             for g in range(Q_PER_KV)], axis=0)
        kb = kbuf[r0:r0 + band, hk * HEAD_DIM:(hk + 1) * HEAD_DIM]
        return lax.dot_general(qs, kb, (((1,), (1,)), ((), ())), preferred_element_type=F32)

    def softmax(c, s):
        b, hk = divmod(c, N_KV_HEADS)
        n = first_block + b
        bias_prev = tri_prev + jnp.where(n == 0, NEG_INF, 0.0)
        bias_next = tri_next + jnp.where(n == blocks_per_seq - 1, NEG_INF, 0.0)
        ps, sink_terms = [], []
        for g in range(Q_PER_KV):
            sg = s[g * BLOCK:(g + 1) * BLOCK]
            s0 = sg[:, 0:BLOCK] + bias_prev
            s1 = sg[:, BLOCK:2 * BLOCK]
            s2 = sg[:, 2 * BLOCK:] + bias_next
            sink = sink_ref[hk * Q_PER_KV + g] * LOG2_E
            m = jnp.maximum(jnp.max(jnp.maximum(jnp.maximum(s0, s1), s2), axis=-1, keepdims=True), sink)
            ps.append(jnp.concatenate([jnp.exp2(s0 - m), jnp.exp2(s1 - m), jnp.exp2(s2 - m)],
                                      axis=1).astype(BF16))
            sink_terms.append(jnp.exp2(sink - m))
        return jnp.concatenate(ps, axis=0), sink_terms

    def weighted_values(c, p, sink_terms):
        b, hk = divmod(c, N_KV_HEADS)
        r0 = b * BLOCK
        oe = jnp.dot(p, vbuf[r0:r0 + band, hk * ext:(hk + 1) * ext], preferred_element_type=F32)
        outs = []
        for g in range(Q_PER_KV):
            og = oe[g * BLOCK:(g + 1) * BLOCK]
            denom = og[:, HEAD_DIM:] + sink_terms[g]
            outs.append(og[:, :HEAD_DIM] * (1.0 / denom))
        return outs

    def finish_block(b, heads):
        r0 = b * BLOCK
        sq = heads[0] * heads[0]
        for t in heads[1:]:
            sq = sq + t * t
        inv = lax.rsqrt(jnp.sum(sq, axis=-1, keepdims=True) * (1.0 / ATTN_WIDTH) + RMS_EPS)
        for hh, t in enumerate(heads):
            sl = slice(hh * HEAD_DIM, (hh + 1) * HEAD_DIM)
            o_ref[r0:r0 + BLOCK, sl] = (t * inv * gain[:, sl]).astype(BF16)

    lead = 2
    s_vals = {c: scores(c) for c in range(min(lead, n_chains))}
    p_vals, heads = {}, {}
    for c in range(n_chains):
        if c + lead < n_chains:
            s_vals[c + lead] = scores(c + lead)
        p_vals[c] = softmax(c, s_vals.pop(c))
        for done in ([c - 1] if c >= 1 else []) + ([c] if c == n_chains - 1 else []):
            b, hk = divmod(done, N_KV_HEADS)
            heads.setdefault(b, []).extend(weighted_values(done, *p_vals.pop(done)))
            if hk == N_KV_HEADS - 1:
                finish_block(b, heads.pop(b))


def _attention(q, k, v, sink, gain, *, seq, q_blocks):
    rows = q.shape[0]
    tq = q_blocks * BLOCK
    bps = seq // BLOCK

    def prev_map(r):
        g0 = r * q_blocks
        return (jnp.where(g0 % bps == 0, g0, g0 - 1), 0)

    def next_map(r):
        g1 = (r + 1) * q_blocks
        return (jnp.where(g1 % bps == 0, g1 - 1, g1), 0)

    edge = pl.BlockSpec((BLOCK, KV_WIDTH), prev_map)
    edge_n = pl.BlockSpec((BLOCK, KV_WIDTH), next_map)
    cur = pl.BlockSpec((tq, KV_WIDTH), lambda r: (r, 0))
    return pl.pallas_call(
        functools.partial(_attn_kernel, q_blocks=q_blocks, blocks_per_seq=bps),
        grid=(rows // tq,),
        in_specs=[
            pl.BlockSpec(memory_space=pltpu.SMEM),
            pl.BlockSpec((tq, ATTN_WIDTH), lambda r: (r, 0)),
            edge, cur, edge_n, edge, cur, edge_n,
            pl.BlockSpec((1, ATTN_WIDTH), lambda r: (0, 0)),
        ],
        out_specs=pl.BlockSpec((tq, ATTN_WIDTH), lambda r: (r, 0)),
        out_shape=jax.ShapeDtypeStruct((rows, ATTN_WIDTH), BF16),
        scratch_shapes=[pltpu.VMEM((tq + 2 * BLOCK, KV_WIDTH), BF16),
                        pltpu.VMEM((tq + 2 * BLOCK, 2 * KV_WIDTH), BF16)],
        compiler_params=pltpu.CompilerParams(dimension_semantics=("parallel",)),
        name="attention",
    )(sink, q, k, k, k, v, v, v, gain.reshape(1, ATTN_WIDTH))


FFT_COLS = 8


SUBLANES = 8


def _pitch(rows):
    return rows + SUBLANES if (rows // SUBLANES) % 2 == 0 else rows


def _fft_kernel(pq_ref, a_ref, tc_ref, ts_ref, c_ref, s_ref, wl_ref, o_ref, p_s, q_s, z_s, *, n_outer):
    in_pitch = _pitch(BLOCK)
    out_pitch = _pitch(n_outer)
    for n1 in range(n_outer):
        rows = slice(n1 * BLOCK, (n1 + 1) * BLOCK)
        p_s[n1 * in_pitch:n1 * in_pitch + BLOCK, :] = pq_ref[0, rows, :].astype(F32)
        q_s[n1 * in_pitch:n1 * in_pitch + BLOCK, :] = pq_ref[1, rows, :].astype(F32)

    def slow_rows(n2):
        return pl.ds(n2, n_outer, stride=in_pitch)

    a_mat = a_ref[...]
    for n2_0 in range(0, BLOCK, FFT_COLS):
        cols = range(n2_0, n2_0 + FFT_COLS)
        x = jnp.concatenate([jnp.concatenate([p_s[slow_rows(n2), :] for n2 in cols], axis=1),
                             jnp.concatenate([q_s[slow_rows(n2), :] for n2 in cols], axis=1)], axis=0)
        y = jnp.dot(a_mat, x.astype(BF16), preferred_element_type=F32)
        for t, n2 in enumerate(cols):
            yr = y[:n_outer, t * LANES:(t + 1) * LANES]
            yi = y[n_outer:, t * LANES:(t + 1) * LANES]
            tc = tc_ref[n2]
            ts = ts_ref[n2]
            p_s[slow_rows(n2), :] = yr * tc + yi * ts
            q_s[slow_rows(n2), :] = yi * tc - yr * ts

    c_mat = c_ref[...]
    s_mat = s_ref[...]
    w_lin = wl_ref[...]
    step = min(FFT_COLS, n_outer)
    for k1_0 in range(0, n_outer, step):
        ks = range(k1_0, k1_0 + step)
        yr = jnp.concatenate([p_s[k1 * in_pitch:k1 * in_pitch + BLOCK, :] for k1 in ks], axis=1).astype(BF16)
        yi = jnp.concatenate([q_s[k1 * in_pitch:k1 * in_pitch + BLOCK, :] for k1 in ks], axis=1).astype(BF16)
        z = (jnp.dot(c_mat, yr, preferred_element_type=F32)
             + jnp.dot(s_mat, yi, preferred_element_type=F32))
        z_rows = jnp.concatenate([z[:, t * LANES:(t + 1) * LANES] for t in range(step)], axis=0).astype(BF16)
        out = jnp.dot(z_rows, w_lin, preferred_element_type=F32)
        for t, k1 in enumerate(ks):
            z_s[pl.ds(k1, BLOCK, stride=out_pitch), :] = out[t * BLOCK:(t + 1) * BLOCK]
    for k2 in range(BLOCK):
        o_ref[k2 * n_outer:(k2 + 1) * n_outer, :] = z_s[k2 * out_pitch:k2 * out_pitch + n_outer, :].astype(o_ref.dtype)


def _fourier_mix(pq, a_mat, tw_cos, tw_sin, c_mat, s_mat, w_lin, *, layer, batch, seq):
    n_outer = seq // BLOCK
    const = functools.partial(pl.BlockSpec, pipeline_mode=pl.Buffered(1))
    return pl.pallas_call(
        functools.partial(_fft_kernel, n_outer=n_outer),
        grid=(batch, N_FOURIER_GROUPS),
        in_specs=[
            pl.BlockSpec((2, seq, LANES), lambda b, g: (0, b, g)),
            const((2 * n_outer, 2 * n_outer), lambda b, g: (0, 0)),
            const((BLOCK, n_outer, LANES), lambda b, g: (0, 0, 0)),
            const((BLOCK, n_outer, LANES), lambda b, g: (0, 0, 0)),
            const((BLOCK, BLOCK), lambda b, g: (0, 0)),
            const((BLOCK, BLOCK), lambda b, g: (0, 0)),
            pl.BlockSpec((None, None, FOURIER_GROUP_DIM, FOURIER_GROUP_DIM), lambda b, g: (layer, g, 0, 0)),
        ],
        out_specs=pl.BlockSpec((seq, LANES), lambda b, g: (b, g)),
        out_shape=jax.ShapeDtypeStruct((batch * seq, FOURIER_WIDTH), BF16),
        scratch_shapes=[pltpu.VMEM((n_outer * _pitch(BLOCK), LANES), F32),
                        pltpu.VMEM((n_outer * _pitch(BLOCK), LANES), F32),
                        pltpu.VMEM((BLOCK * _pitch(n_outer), LANES), F32)],
        compiler_params=pltpu.CompilerParams(
            dimension_semantics=("parallel", "parallel"), vmem_limit_bytes=V7X_VMEM_LIMIT_BYTES),
        name="fourier_mix",
    )(pq, a_mat, tw_cos, tw_sin, c_mat, s_mat, w_lin)


def _outproj_kernel(*refs, x_ranges):
    x_refs = refs[:len(x_ranges)]
    a_ref, f_ref, mod_ref, post_g_ref, fg_ref, w_ref, o_ref, fn_ref, mult_ref = refs[len(x_ranges):]
    rows = f_ref.shape[0]
    rc = min(ROW_CHUNK, rows)
    for r0 in range(0, rows, rc):
        acc = None
        for c0 in range(0, FOURIER_WIDTH, LANES):
            t = f_ref[r0:r0 + rc, c0:c0 + LANES].astype(F32)
            acc = t * t if acc is None else acc + t * t
        inv = lax.rsqrt(jnp.sum(acc, axis=-1, keepdims=True) * (1.0 / FOURIER_WIDTH) + RMS_EPS)
        for c0 in range(0, FOURIER_WIDTH, LANES):
            sl = slice(c0, c0 + LANES)
            fn_ref[r0:r0 + rc, sl] = (f_ref[r0:r0 + rc, sl].astype(F32) * inv * fg_ref[:, sl]).astype(BF16)
    o_ref[...] = (jnp.dot(a_ref[...], w_ref[:ATTN_WIDTH, :], preferred_element_type=F32)
                  + jnp.dot(fn_ref[...], w_ref[ATTN_WIDTH:, :], preferred_element_type=F32))
    _with_owner(x_refs, x_ranges, lambda x_ref: _post_into(o_ref, x_ref, post_g_ref, mod_ref, mult_ref, 1.0))


def _outproj(a, f, x_parts, mod, post_g, f_gain, w_out, *, layer, seq, tm):
    rows, d = sum(p.shape[0] for p in x_parts), x_parts[0].shape[1]
    tiles_per_seq = seq // tm
    x_specs, x_ranges = _part_specs(x_parts, tm)
    return pl.pallas_call(
        functools.partial(_outproj_kernel, x_ranges=x_ranges),
        grid=(rows // tm,),
        in_specs=x_specs + [
            pl.BlockSpec((tm, ATTN_WIDTH), lambda r: (r, 0)),
            pl.BlockSpec((tm, FOURIER_WIDTH), lambda r: (r, 0)),
            pl.BlockSpec((1, N_MOD, d), lambda r: (r // tiles_per_seq, 0, 0)),
            pl.BlockSpec((1, d), lambda r: (0, 0)),
            pl.BlockSpec((1, FOURIER_WIDTH), lambda r: (0, 0)),
            pl.BlockSpec((None, ATTN_WIDTH + FOURIER_WIDTH, d), lambda r: (layer, 0, 0)),
        ],
        out_specs=pl.BlockSpec((tm, d), lambda r: (r, 0)),
        out_shape=jax.ShapeDtypeStruct((rows, d), F32),
        scratch_shapes=[pltpu.VMEM((tm, FOURIER_WIDTH), BF16), pltpu.VMEM((1, d), F32)],
        compiler_params=pltpu.CompilerParams(
            dimension_semantics=("parallel",), vmem_limit_bytes=V7X_VMEM_LIMIT_BYTES),
        name="outproj",
    )(*x_parts, a, f, mod, post_g.reshape(1, d), f_gain.reshape(1, FOURIER_WIDTH), w_out)


def _rope_tables(seq):
    inv_freq = ROPE_THETA ** (-jnp.arange(0, HEAD_DIM, 2, dtype=F32) / HEAD_DIM)
    ang = jnp.arange(seq, dtype=F32)[:, None] * inv_freq[None, :]
    cos, sin = jnp.cos(ang), jnp.sin(ang)
    return jnp.concatenate([cos, cos], axis=-1), jnp.concatenate([-sin, sin], axis=-1)


def _dft_tables(seq):
    n_outer = seq // BLOCK

    def cs(n, scale):
        idx = np.arange(n)
        ang = 2.0 * np.pi * ((idx[:, None] * idx[None, :]) % n) / n
        return np.cos(ang) * scale, np.sin(ang) * scale

    cc, sc = cs(FOURIER_GROUP_DIM, FOURIER_GROUP_DIM ** -0.5)
    wc = np.concatenate([cc, sc], axis=1)
    co, so = cs(n_outer, n_outer ** -0.5)
    a_mat = np.block([[co, -so], [-so, -co]])
    c128, s128 = cs(BLOCK, BLOCK ** -0.5)
    n2 = np.arange(BLOCK)[:, None]
    k1 = np.arange(n_outer)[None, :]
    tw = 2.0 * np.pi * ((n2 * k1) % seq) / seq
    tw_cos = np.broadcast_to(np.cos(tw)[:, :, None], (BLOCK, n_outer, 128))
    tw_sin = np.broadcast_to(np.sin(tw)[:, :, None], (BLOCK, n_outer, 128))
    as_f32 = lambda a: jnp.asarray(np.ascontiguousarray(a), dtype=F32)
    return (as_f32(wc).astype(BF16), as_f32(a_mat).astype(BF16), as_f32(tw_cos), as_f32(tw_sin),
            as_f32(c128).astype(BF16), as_f32(s128).astype(BF16))


def _tile(seq, want):
    return min(seq, want)


def _trunk(x_groups, c_groups, w_mod, b_mod, pre_g, post_g, ffn_w_gate, ffn_w_up, ffn_w_down,
           w_in, attn_sink, fourier_w, branch_g, w_out):
    seq, d = x_groups[0].shape[1:]
    sizes = [x.shape[0] for x in x_groups]
    batch = sum(sizes)
    depth = w_mod.shape[0]
    assert seq % BLOCK == 0 and batch <= MOD_ROWS and all(x.shape[1:] == (seq, d) for x in x_groups)
    n_outer = seq // BLOCK

    c_pad = jnp.zeros((MOD_ROWS, d), F32).at[:batch].set(jnp.concatenate(c_groups, axis=0))
    mod = _modulation(c_pad, w_mod, b_mod).reshape(depth, MOD_ROWS, N_SUBLAYERS, N_MOD, d)

    cos_t, sin_t = _rope_tables(seq)
    wc, a_mat, tw_cos, tw_sin, c128, s128 = _dft_tables(seq)

    wg = ffn_w_gate.astype(BF16)
    wu = ffn_w_up.astype(BF16)
    wd = ffn_w_down.astype(BF16)
    w_in_b = w_in.astype(BF16)
    w_out_b = w_out.astype(BF16)
    w_lin = fourier_w.astype(BF16)

    tm_ffn = _tile(seq, 1024)
    tf = 512 if wg.shape[-1] % 512 == 0 else wg.shape[-1]
    tm_proj = _tile(seq, 512)
    q_blocks = min(8, n_outer)
    tiles_per_seq = seq // tm_ffn
    group_tile0 = [sum(sizes[:i]) * tiles_per_seq for i in range(len(sizes))]

    def ffn(x, l, sub, which, **kw):
        return _ffn(x, mod[l, :, sub], pre_g[l, sub], post_g[l, sub], wg, wu, wd, layer=l, which=which,
                    seq=seq, weight=0.5, tm=tm_ffn, tf=tf, **kw)

    xs = None
    for l in range(depth):
        if l == 0:
            x_parts = [ffn(x.reshape(n * seq, d), l, 0, 0, seq_tile0=t0)
                       for x, n, t0 in zip(x_groups, sizes, group_tile0)]
        else:
            x_parts = [ffn(xs, l, 0, 0)]
        q, k, v, pq = _inproj(x_parts, mod[l, :, 1], pre_g[l, 1], w_in_b, cos_t, sin_t, wc,
                              layer=l, seq=seq, tm=tm_proj)
        a_out = _attention(q, k, v, attn_sink[l], branch_g[l, 0], seq=seq, q_blocks=q_blocks)
        f_raw = _fourier_mix(pq, a_mat, tw_cos, tw_sin, c128, s128, w_lin, layer=l, batch=batch, seq=seq)
        xs = _outproj(a_out, f_raw, x_parts, mod[l, :, 1], post_g[l, 1], branch_g[l, 1], w_out_b,
                      layer=l, seq=seq, tm=tm_proj)
        if l < depth - 1:
            xs = ffn(xs, l, 2, 1)
    outs = []
    for n, t0 in zip(sizes, group_tile0):
        y = ffn(xs, depth - 1, 2, 1, tiles=n * tiles_per_seq, in_tile0=t0, seq_tile0=t0)
        outs.append(y.reshape(n, seq, d))
    return outs


def kernel(x_prompt, x_sample, c_prompt, c_sample, w_mod, b_mod, pre_g, post_g, ffn_w_gate, ffn_w_up,
           ffn_w_down, w_in, attn_sink, fourier_w, branch_g, w_out):
    y_prompt, y_sample = _trunk([x_prompt, x_sample], [c_prompt, c_sample], w_mod, b_mod, pre_g, post_g,
                                ffn_w_gate, ffn_w_up, ffn_w_down, w_in, attn_sink, fourier_w, branch_g, w_out)
    return y_prompt, y_sample
```

```python
import functools

import numpy as np
import jax
import jax.numpy as jnp
from jax import lax
from jax.experimental import pallas as pl
from jax.experimental.pallas import tpu as pltpu

HEAD_DIM = 128
N_Q_HEADS = 8
N_KV_HEADS = 2
Q_PER_KV = N_Q_HEADS // N_KV_HEADS
ATTN_WIDTH = N_Q_HEADS * HEAD_DIM
KV_WIDTH = N_KV_HEADS * HEAD_DIM
N_FOURIER_GROUPS = 8
FOURIER_GROUP_DIM = 128
FOURIER_WIDTH = N_FOURIER_GROUPS * FOURIER_GROUP_DIM
WINDOW = 128
BLOCK = 128
ROPE_THETA = 10000.0
N_SUBLAYERS = 3
N_MOD = 3
RMS_EPS = 1e-6
NEG_INF = -1e30
LOG2_E = 1.4426950408889634
MOD_ROWS = 8

V7X_VMEM_LIMIT_BYTES = 60 * 1024 * 1024

BF16 = jnp.bfloat16
F32 = jnp.float32


def _sigmoid(x):
    return 1.0 / (1.0 + jnp.exp(-x))


LANES = 128
ROW_CHUNK = 64


def _row_rms_scale(ref, r0, rows):
    d = ref.shape[-1]
    acc = None
    for c0 in range(0, d, LANES):
        t = ref[r0:r0 + rows, c0:c0 + LANES]
        acc = t * t if acc is None else acc + t * t
    return lax.rsqrt(jnp.sum(acc, axis=-1, keepdims=True) * (1.0 / d) + RMS_EPS)


def _pre_into(h_ref, x_ref, gain_ref, mod_ref, mult_ref):
    rows, d = x_ref.shape
    rc = min(ROW_CHUNK, rows)
    mult_ref[...] = gain_ref[...] * (1.0 + mod_ref[0, 1:2, :])
    for r0 in range(0, rows, rc):
        inv = _row_rms_scale(x_ref, r0, rc)
        for c0 in range(0, d, LANES):
            sl = slice(c0, c0 + LANES)
            t = (x_ref[r0:r0 + rc, sl] * inv) * mult_ref[:, sl] + mod_ref[0, 0:1, sl]
            h_ref[r0:r0 + rc, sl] = t.astype(h_ref.dtype)


def _post_into(o_ref, x_ref, gain_ref, mod_ref, mult_ref, weight):
    rows, d = x_ref.shape
    rc = min(ROW_CHUNK, rows)
    mult_ref[...] = (weight * (1.0 + mod_ref[0, 2:3, :])) * gain_ref[...]
    for r0 in range(0, rows, rc):
        inv = _row_rms_scale(o_ref, r0, rc)
        for c0 in range(0, d, LANES):
            sl = slice(c0, c0 + LANES)
            y = o_ref[r0:r0 + rc, sl]
            o_ref[r0:r0 + rc, sl] = x_ref[r0:r0 + rc, sl] + (y * inv) * mult_ref[:, sl]


def _part_specs(parts, tm):
    specs, ranges, t0 = [], [], 0
    for p in parts:
        n = p.shape[0] // tm
        specs.append(pl.BlockSpec((tm, p.shape[1]), lambda r, t0=t0, n=n: (jnp.clip(r - t0, 0, n - 1), 0)))
        ranges.append((t0, t0 + n))
        t0 += n
    return specs, ranges


def _with_owner(refs, ranges, fn):
    if len(refs) == 1:
        fn(refs[0])
        return
    r = pl.program_id(0)
    for ref, (lo, hi) in zip(refs, ranges):
        pl.when((r >= lo) & (r < hi))(functools.partial(fn, ref))


def _mod_kernel(c_ref, w_ref, b_ref, o_ref):
    c = c_ref[...]
    act = (c * _sigmoid(c)).astype(BF16)
    o_ref[0] = jnp.dot(act, w_ref[0].astype(BF16), preferred_element_type=F32) + b_ref[0]


def _modulation(c_pad, w_mod, b_mod):
    depth, d, width = w_mod.shape
    tn = 1024
    return pl.pallas_call(
        _mod_kernel,
        grid=(depth, width // tn),
        in_specs=[
            pl.BlockSpec((MOD_ROWS, d), lambda l, n: (0, 0)),
            pl.BlockSpec((1, d, tn), lambda l, n: (l, 0, n)),
            pl.BlockSpec((1, 1, tn), lambda l, n: (l, 0, n)),
        ],
        out_specs=pl.BlockSpec((1, MOD_ROWS, tn), lambda l, n: (l, 0, n)),
        out_shape=jax.ShapeDtypeStruct((depth, MOD_ROWS, width), F32),
        compiler_params=pltpu.CompilerParams(dimension_semantics=("parallel", "parallel")),
        name="modulation",
    )(c_pad, w_mod, b_mod.reshape(depth, 1, width))


def _ffn_kernel(x_ref, mod_ref, pre_g_ref, post_g_ref, wg_ref, wu_ref, wd_ref, o_ref, h_ref, mult_ref,
                *, n_chunks, weight):
    j = pl.program_id(1)

    def down_projection():
        h = h_ref[...]
        tf = wg_ref.shape[1]
        halves = [slice(c0, c0 + tf // 2) for c0 in (0, tf // 2)]
        gu = [(jnp.dot(h, wg_ref[:, sl], preferred_element_type=F32),
               jnp.dot(h, wu_ref[:, sl], preferred_element_type=F32)) for sl in halves]
        down = None
        for sl, (g, u) in zip(halves, gu):
            a = ((g * _sigmoid(g)) * u).astype(BF16)
            part = jnp.dot(a, wd_ref[sl, :], preferred_element_type=F32)
            down = part if down is None else down + part
        return down

    @pl.when(j == 0)
    def _():
        _pre_into(h_ref, x_ref, pre_g_ref, mod_ref, mult_ref)
        o_ref[...] = down_projection()

    @pl.when(j > 0)
    def _():
        o_ref[...] += down_projection()

    @pl.when(j == n_chunks - 1)
    def _():
        _post_into(o_ref, x_ref, post_g_ref, mod_ref, mult_ref, weight)


def _ffn(x, mod, pre_g, post_g, wg, wu, wd, *, layer, which, seq, weight, tm, tf,
         tiles=None, in_tile0=0, seq_tile0=0):
    d = x.shape[1]
    d_ff = wg.shape[-1]
    tiles = x.shape[0] // tm if tiles is None else tiles
    tiles_per_seq = seq // tm
    n_chunks = d_ff // tf
    return pl.pallas_call(
        functools.partial(_ffn_kernel, n_chunks=n_chunks, weight=weight),
        grid=(tiles, n_chunks),
        in_specs=[
            pl.BlockSpec((tm, d), lambda r, j: (r + in_tile0, 0)),
            pl.BlockSpec((1, N_MOD, d), lambda r, j: ((r + seq_tile0) // tiles_per_seq, 0, 0)),
            pl.BlockSpec((1, d), lambda r, j: (0, 0)),
            pl.BlockSpec((1, d), lambda r, j: (0, 0)),
            pl.BlockSpec((None, None, d, tf), lambda r, j: (layer, which, 0, j)),
            pl.BlockSpec((None, None, d, tf), lambda r, j: (layer, which, 0, j)),
            pl.BlockSpec((None, None, tf, d), lambda r, j: (layer, which, j, 0)),
        ],
        out_specs=pl.BlockSpec((tm, d), lambda r, j: (r, 0)),
        out_shape=jax.ShapeDtypeStruct((tiles * tm, d), F32),
        scratch_shapes=[pltpu.VMEM((tm, d), BF16), pltpu.VMEM((1, d), F32)],
        compiler_params=pltpu.CompilerParams(
            dimension_semantics=("parallel", "arbitrary"),
            vmem_limit_bytes=V7X_VMEM_LIMIT_BYTES),
        name="ffn",
    )(x, mod, pre_g.reshape(1, d), post_g.reshape(1, d), wg, wu, wd)


def _inproj_kernel(*refs, x_ranges):
    x_refs = refs[:len(x_ranges)]
    (mod_ref, pre_g_ref, w_ref, cos_ref, sin_ref, wc_ref,
     q_ref, k_ref, v_ref, pq_ref, h_ref, mult_ref) = refs[len(x_ranges):]

    def body(x_ref):
        _pre_into(h_ref, x_ref, pre_g_ref, mod_ref, mult_ref)
        proj = jnp.dot(h_ref[...], w_ref[...], preferred_element_type=F32)
        cos_t = cos_ref[...]
        sin_t = sin_ref[...]

        def rope(t):
            return t * cos_t + pltpu.roll(t, HEAD_DIM // 2, 1) * sin_t

        q_scale = LOG2_E * HEAD_DIM ** -0.5
        for hh in range(N_Q_HEADS):
            sl = slice(hh * HEAD_DIM, (hh + 1) * HEAD_DIM)
            q_ref[:, sl] = (rope(proj[:, sl]) * q_scale).astype(BF16)
        for hh in range(N_KV_HEADS):
            src = slice(ATTN_WIDTH + hh * HEAD_DIM, ATTN_WIDTH + (hh + 1) * HEAD_DIM)
            k_ref[:, hh * HEAD_DIM:(hh + 1) * HEAD_DIM] = rope(proj[:, src]).astype(BF16)
        v_ref[...] = proj[:, ATTN_WIDTH + KV_WIDTH:ATTN_WIDTH + 2 * KV_WIDTH].astype(BF16)
        u0 = ATTN_WIDTH + 2 * KV_WIDTH
        wc = wc_ref[...]
        for g in range(N_FOURIER_GROUPS):
            dst = slice(g * FOURIER_GROUP_DIM, (g + 1) * FOURIER_GROUP_DIM)
            ug = proj[:, u0 + g * FOURIER_GROUP_DIM:u0 + (g + 1) * FOURIER_GROUP_DIM].astype(BF16)
            pq = jnp.dot(ug, wc, preferred_element_type=F32)
            pq_ref[0, :, dst] = pq[:, :FOURIER_GROUP_DIM].astype(BF16)
            pq_ref[1, :, dst] = pq[:, FOURIER_GROUP_DIM:].astype(BF16)

    _with_owner(x_refs, x_ranges, body)


def _inproj(x_parts, mod, pre_g, w_in, cos_t, sin_t, wc, *, layer, seq, tm):
    rows, d = sum(p.shape[0] for p in x_parts), x_parts[0].shape[1]
    tiles_per_seq = seq // tm
    width = w_in.shape[-1]
    x_specs, x_ranges = _part_specs(x_parts, tm)
    return pl.pallas_call(
        functools.partial(_inproj_kernel, x_ranges=x_ranges),
        grid=(rows // tm,),
        in_specs=x_specs + [
            pl.BlockSpec((1, N_MOD, d), lambda r: (r // tiles_per_seq, 0, 0)),
            pl.BlockSpec((1, d), lambda r: (0, 0)),
            pl.BlockSpec((None, d, width), lambda r: (layer, 0, 0)),
            pl.BlockSpec((tm, HEAD_DIM), lambda r: (r % tiles_per_seq, 0)),
            pl.BlockSpec((tm, HEAD_DIM), lambda r: (r % tiles_per_seq, 0)),
            pl.BlockSpec((FOURIER_GROUP_DIM, 2 * FOURIER_GROUP_DIM), lambda r: (0, 0)),
        ],
        out_specs=[
            pl.BlockSpec((tm, ATTN_WIDTH), lambda r: (r, 0)),
            pl.BlockSpec((tm, KV_WIDTH), lambda r: (r, 0)),
            pl.BlockSpec((tm, KV_WIDTH), lambda r: (r, 0)),
            pl.BlockSpec((2, tm, FOURIER_WIDTH), lambda r: (0, r, 0)),
        ],
        out_shape=[
            jax.ShapeDtypeStruct((rows, ATTN_WIDTH), BF16),
            jax.ShapeDtypeStruct((rows, KV_WIDTH), BF16),
            jax.ShapeDtypeStruct((rows, KV_WIDTH), BF16),
            jax.ShapeDtypeStruct((2, rows, FOURIER_WIDTH), BF16),
        ],
        scratch_shapes=[pltpu.VMEM((tm, d), BF16), pltpu.VMEM((1, d), F32)],
        compiler_params=pltpu.CompilerParams(
            dimension_semantics=("parallel",), vmem_limit_bytes=V7X_VMEM_LIMIT_BYTES),
        name="inproj",
    )(*x_parts, mod, pre_g.reshape(1, d), w_in, cos_t, sin_t, wc)


def _attn_kernel(sink_ref, q_ref, kp_ref, kc_ref, kn_ref, vp_ref, vc_ref, vn_ref, g_ref,
                 o_ref, kbuf, vbuf, *, q_blocks, blocks_per_seq):
    tq = q_blocks * BLOCK
    band = 3 * BLOCK
    ext = 2 * HEAD_DIM
    kbuf[0:BLOCK] = kp_ref[...]
    kbuf[BLOCK:BLOCK + tq] = kc_ref[...]
    kbuf[BLOCK + tq:2 * BLOCK + tq] = kn_ref[...]
    for hk in range(N_KV_HEADS):
        src = slice(hk * HEAD_DIM, (hk + 1) * HEAD_DIM)
        dst = slice(hk * ext, hk * ext + HEAD_DIM)
        vbuf[0:BLOCK, dst] = vp_ref[:, src]
        vbuf[BLOCK:BLOCK + tq, dst] = vc_ref[:, src]
        vbuf[BLOCK + tq:2 * BLOCK + tq, dst] = vn_ref[:, src]
        vbuf[:, hk * ext + HEAD_DIM:(hk + 1) * ext] = jnp.ones((tq + 2 * BLOCK, HEAD_DIM), BF16)

    first_block = (pl.program_id(0) * q_blocks) % blocks_per_seq
    qi = lax.broadcasted_iota(jnp.int32, (BLOCK, BLOCK), 0)
    kj = lax.broadcasted_iota(jnp.int32, (BLOCK, BLOCK), 1)
    tri_prev = jnp.where(kj >= qi, 0.0, NEG_INF)
    tri_next = jnp.where(kj <= qi, 0.0, NEG_INF)
    gain = g_ref[...]
    n_chains = q_blocks * N_KV_HEADS

    def scores(c):
        b, hk = divmod(c, N_KV_HEADS)
        r0 = b * BLOCK
        qs = jnp.concatenate(
            [q_ref[r0:r0 + BLOCK, (hk * Q_PER_KV + g) * HEAD_DIM:(hk * Q_PER_KV + g + 1) * HEAD_DIM]
             for g in range(Q_PER_KV)], axis=0)
        kb = kbuf[r0:r0 + band, hk * HEAD_DIM:(hk + 1) * HEAD_DIM]
        return lax.dot_general(qs, kb, (((1,), (1,)), ((), ())), preferred_element_type=F32)

    def softmax(c, s):
        b, hk = divmod(c, N_KV_HEADS)
        n = first_block + b
        bias_prev = tri_prev + jnp.where(n == 0, NEG_INF, 0.0)
        bias_next = tri_next + jnp.where(n == blocks_per_seq - 1, NEG_INF, 0.0)
        ps, sink_terms = [], []
        for g in range(Q_PER_KV):
            sg = s[g * BLOCK:(g + 1) * BLOCK]
            s0 = sg[:, 0:BLOCK] + bias_prev
            s1 = sg[:, BLOCK:2 * BLOCK]
            s2 = sg[:, 2 * BLOCK:] + bias_next
            sink = sink_ref[hk * Q_PER_KV + g] * LOG2_E
            m = jnp.maximum(jnp.max(jnp.maximum(jnp.maximum(s0, s1), s2), axis=-1, keepdims=True), sink)
            ps.append(jnp.concatenate([jnp.exp2(s0 - m), jnp.exp2(s1 - m), jnp.exp2(s2 - m)],
                                      axis=1).astype(BF16))
            sink_terms.append(jnp.exp2(sink - m))
        return jnp.concatenate(ps, axis=0), sink_terms

    def weighted_values(c, p, sink_terms):
        b, hk = divmod(c, N_KV_HEADS)
        r0 = b * BLOCK
        oe = jnp.dot(p, vbuf[r0:r0 + band, hk * ext:(hk + 1) * ext], preferred_element_type=F32)
        outs = []
        for g in range(Q_PER_KV):
            og = oe[g * BLOCK:(g + 1) * BLOCK]
            denom = og[:, HEAD_DIM:] + sink_terms[g]
            outs.append(og[:, :HEAD_DIM] * (1.0 / denom))
        return outs

    def finish_block(b, heads):
        r0 = b * BLOCK
        sq = heads[0] * heads[0]
        for t in heads[1:]:
            sq = sq + t * t
        inv = lax.rsqrt(jnp.sum(sq, axis=-1, keepdims=True) * (1.0 / ATTN_WIDTH) + RMS_EPS)
        for hh, t in enumerate(heads):
            sl = slice(hh * HEAD_DIM, (hh + 1) * HEAD_DIM)
            o_ref[r0:r0 + BLOCK, sl] = (t * inv * gain[:, sl]).astype(BF16)

    lead = 2
    s_vals = {c: scores(c) for c in range(min(lead, n_chains))}
    p_vals, heads = {}, {}
    for c in range(n_chains):
        if c + lead < n_chains:
            s_vals[c + lead] = scores(c + lead)
        p_vals[c] = softmax(c, s_vals.pop(c))
        for done in ([c - 1] if c >= 1 else []) + ([c] if c == n_chains - 1 else []):
            b, hk = divmod(done, N_KV_HEADS)
            heads.setdefault(b, []).extend(weighted_values(done, *p_vals.pop(done)))
            if hk == N_KV_HEADS - 1:
                finish_block(b, heads.pop(b))


def _attention(q, k, v, sink, gain, *, seq, q_blocks):
    rows = q.shape[0]
    tq = q_blocks * BLOCK
    bps = seq // BLOCK

    def prev_map(r):
        g0 = r * q_blocks
        return (jnp.where(g0 % bps == 0, g0, g0 - 1), 0)

    def next_map(r):
        g1 = (r + 1) * q_blocks
        return (jnp.where(g1 % bps == 0, g1 - 1, g1), 0)

    edge = pl.BlockSpec((BLOCK, KV_WIDTH), prev_map)
    edge_n = pl.BlockSpec((BLOCK, KV_WIDTH), next_map)
    cur = pl.BlockSpec((tq, KV_WIDTH), lambda r: (r, 0))
    return pl.pallas_call(
        functools.partial(_attn_kernel, q_blocks=q_blocks, blocks_per_seq=bps),
        grid=(rows // tq,),
        in_specs=[
            pl.BlockSpec(memory_space=pltpu.SMEM),
            pl.BlockSpec((tq, ATTN_WIDTH), lambda r: (r, 0)),
            edge, cur, edge_n, edge, cur, edge_n,
            pl.BlockSpec((1, ATTN_WIDTH), lambda r: (0, 0)),
        ],
        out_specs=pl.BlockSpec((tq, ATTN_WIDTH), lambda r: (r, 0)),
        out_shape=jax.ShapeDtypeStruct((rows, ATTN_WIDTH), BF16),
        scratch_shapes=[pltpu.VMEM((tq + 2 * BLOCK, KV_WIDTH), BF16),
                        pltpu.VMEM((tq + 2 * BLOCK, 2 * KV_WIDTH), BF16)],
        compiler_params=pltpu.CompilerParams(dimension_semantics=("parallel",)),
        name="attention",
    )(sink, q, k, k, k, v, v, v, gain.reshape(1, ATTN_WIDTH))


FFT_COLS = 8


SUBLANES = 8


def _pitch(rows):
    return rows + SUBLANES if (rows // SUBLANES) % 2 == 0 else rows


def _fft_kernel(pq_ref, a_ref, tc_ref, ts_ref, c_ref, s_ref, wl_ref, o_ref, p_s, q_s, z_s, *, n_outer):
    in_pitch = _pitch(BLOCK)
    out_pitch = _pitch(n_outer)
    for n1 in range(n_outer):
        rows = slice(n1 * BLOCK, (n1 + 1) * BLOCK)
        p_s[n1 * in_pitch:n1 * in_pitch + BLOCK, :] = pq_ref[0, rows, :].astype(F32)
        q_s[n1 * in_pitch:n1 * in_pitch + BLOCK, :] = pq_ref[1, rows, :].astype(F32)

    def slow_rows(n2):
        return pl.ds(n2, n_outer, stride=in_pitch)

    a_mat = a_ref[...]
    for n2_0 in range(0, BLOCK, FFT_COLS):
        cols = range(n2_0, n2_0 + FFT_COLS)
        x = jnp.concatenate([jnp.concatenate([p_s[slow_rows(n2), :] for n2 in cols], axis=1),
                             jnp.concatenate([q_s[slow_rows(n2), :] for n2 in cols], axis=1)], axis=0)
        y = jnp.dot(a_mat, x.astype(BF16), preferred_element_type=F32)
        for t, n2 in enumerate(cols):
            yr = y[:n_outer, t * LANES:(t + 1) * LANES]
            yi = y[n_outer:, t * LANES:(t + 1) * LANES]
            tc = tc_ref[n2]
            ts = ts_ref[n2]
            p_s[slow_rows(n2), :] = yr * tc + yi * ts
            q_s[slow_rows(n2), :] = yi * tc - yr * ts

    c_mat = c_ref[...]
    s_mat = s_ref[...]
    w_lin = wl_ref[...]
    step = min(FFT_COLS, n_outer)
    for k1_0 in range(0, n_outer, step):
        ks = range(k1_0, k1_0 + step)
        yr = jnp.concatenate([p_s[k1 * in_pitch:k1 * in_pitch + BLOCK, :] for k1 in ks], axis=1).astype(BF16)
        yi = jnp.concatenate([q_s[k1 * in_pitch:k1 * in_pitch + BLOCK, :] for k1 in ks], axis=1).astype(BF16)
        z = (jnp.dot(c_mat, yr, preferred_element_type=F32)
             + jnp.dot(s_mat, yi, preferred_element_type=F32))
        z_rows = jnp.concatenate([z[:, t * LANES:(t + 1) * LANES] for t in range(step)], axis=0).astype(BF16)
        out = jnp.dot(z_rows, w_lin, preferred_element_type=F32)
        for t, k1 in enumerate(ks):
            z_s[pl.ds(k1, BLOCK, stride=out_pitch), :] = out[t * BLOCK:(t + 1) * BLOCK]
    for k2 in range(BLOCK):
        o_ref[k2 * n_outer:(k2 + 1) * n_outer, :] = z_s[k2 * out_pitch:k2 * out_pitch + n_outer, :].astype(o_ref.dtype)


def _fourier_mix(pq, a_mat, tw_cos, tw_sin, c_mat, s_mat, w_lin, *, layer, batch, seq):
    n_outer = seq // BLOCK
    const = functools.partial(pl.BlockSpec, pipeline_mode=pl.Buffered(1))
    return pl.pallas_call(
        functools.partial(_fft_kernel, n_outer=n_outer),
        grid=(batch, N_FOURIER_GROUPS),
        in_specs=[
            pl.BlockSpec((2, seq, LANES), lambda b, g: (0, b, g)),
            const((2 * n_outer, 2 * n_outer), lambda b, g: (0, 0)),
            const((BLOCK, n_outer, LANES), lambda b, g: (0, 0, 0)),
            const((BLOCK, n_outer, LANES), lambda b, g: (0, 0, 0)),
            const((BLOCK, BLOCK), lambda b, g: (0, 0)),
            const((BLOCK, BLOCK), lambda b, g: (0, 0)),
            pl.BlockSpec((None, None, FOURIER_GROUP_DIM, FOURIER_GROUP_DIM), lambda b, g: (layer, g, 0, 0)),
        ],
        out_specs=pl.BlockSpec((seq, LANES), lambda b, g: (b, g)),
        out_shape=jax.ShapeDtypeStruct((batch * seq, FOURIER_WIDTH), BF16),
        scratch_shapes=[pltpu.VMEM((n_outer * _pitch(BLOCK), LANES), F32),
                        pltpu.VMEM((n_outer * _pitch(BLOCK), LANES), F32),
                        pltpu.VMEM((BLOCK * _pitch(n_outer), LANES), F32)],
        compiler_params=pltpu.CompilerParams(
            dimension_semantics=("parallel", "parallel"), vmem_limit_bytes=V7X_VMEM_LIMIT_BYTES),
        name="fourier_mix",
    )(pq, a_mat, tw_cos, tw_sin, c_mat, s_mat, w_lin)


def _outproj_kernel(*refs, x_ranges):
    x_refs = refs[:len(x_ranges)]
    a_ref, f_ref, mod_ref, post_g_ref, fg_ref, w_ref, o_ref, fn_ref, mult_ref = refs[len(x_ranges):]

    def body(x_ref):
        rows = f_ref.shape[0]
        rc = min(ROW_CHUNK, rows)
        for r0 in range(0, rows, rc):
            acc = None
            for c0 in range(0, FOURIER_WIDTH, LANES):
                t = f_ref[r0:r0 + rc, c0:c0 + LANES].astype(F32)
                acc = t * t if acc is None else acc + t * t
            inv = lax.rsqrt(jnp.sum(acc, axis=-1, keepdims=True) * (1.0 / FOURIER_WIDTH) + RMS_EPS)
            for c0 in range(0, FOURIER_WIDTH, LANES):
                sl = slice(c0, c0 + LANES)
                fn_ref[r0:r0 + rc, sl] = (f_ref[r0:r0 + rc, sl].astype(F32) * inv * fg_ref[:, sl]).astype(BF16)
        o_ref[...] = (jnp.dot(a_ref[...], w_ref[:ATTN_WIDTH, :], preferred_element_type=F32)
                      + jnp.dot(fn_ref[...], w_ref[ATTN_WIDTH:, :], preferred_element_type=F32))
        _post_into(o_ref, x_ref, post_g_ref, mod_ref, mult_ref, 1.0)

    _with_owner(x_refs, x_ranges, body)


def _outproj(a, f, x_parts, mod, post_g, f_gain, w_out, *, layer, seq, tm):
    rows, d = sum(p.shape[0] for p in x_parts), x_parts[0].shape[1]
    tiles_per_seq = seq // tm
    x_specs, x_ranges = _part_specs(x_parts, tm)
    return pl.pallas_call(
        functools.partial(_outproj_kernel, x_ranges=x_ranges),
        grid=(rows // tm,),
        in_specs=x_specs + [
            pl.BlockSpec((tm, ATTN_WIDTH), lambda r: (r, 0)),
            pl.BlockSpec((tm, FOURIER_WIDTH), lambda r: (r, 0)),
            pl.BlockSpec((1, N_MOD, d), lambda r: (r // tiles_per_seq, 0, 0)),
            pl.BlockSpec((1, d), lambda r: (0, 0)),
            pl.BlockSpec((1, FOURIER_WIDTH), lambda r: (0, 0)),
            pl.BlockSpec((None, ATTN_WIDTH + FOURIER_WIDTH, d), lambda r: (layer, 0, 0)),
        ],
        out_specs=pl.BlockSpec((tm, d), lambda r: (r, 0)),
        out_shape=jax.ShapeDtypeStruct((rows, d), F32),
        scratch_shapes=[pltpu.VMEM((tm, FOURIER_WIDTH), BF16), pltpu.VMEM((1, d), F32)],
        compiler_params=pltpu.CompilerParams(
            dimension_semantics=("parallel",), vmem_limit_bytes=V7X_VMEM_LIMIT_BYTES),
        name="outproj",
    )(*x_parts, a, f, mod, post_g.reshape(1, d), f_gain.reshape(1, FOURIER_WIDTH), w_out)


def _rope_tables(seq):
    inv_freq = ROPE_THETA ** (-jnp.arange(0, HEAD_DIM, 2, dtype=F32) / HEAD_DIM)
    ang = jnp.arange(seq, dtype=F32)[:, None] * inv_freq[None, :]
    cos, sin = jnp.cos(ang), jnp.sin(ang)
    return jnp.concatenate([cos, cos], axis=-1), jnp.concatenate([-sin, sin], axis=-1)


def _dft_tables(seq):
    n_outer = seq // BLOCK

    def cs(n, scale):
        idx = np.arange(n)
        ang = 2.0 * np.pi * ((idx[:, None] * idx[None, :]) % n) / n
        return np.cos(ang) * scale, np.sin(ang) * scale

    cc, sc = cs(FOURIER_GROUP_DIM, FOURIER_GROUP_DIM ** -0.5)
    wc = np.concatenate([cc, sc], axis=1)
    co, so = cs(n_outer, n_outer ** -0.5)
    a_mat = np.block([[co, -so], [-so, -co]])
    c128, s128 = cs(BLOCK, BLOCK ** -0.5)
    n2 = np.arange(BLOCK)[:, None]
    k1 = np.arange(n_outer)[None, :]
    tw = 2.0 * np.pi * ((n2 * k1) % seq) / seq
    tw_cos = np.broadcast_to(np.cos(tw)[:, :, None], (BLOCK, n_outer, 128))
    tw_sin = np.broadcast_to(np.sin(tw)[:, :, None], (BLOCK, n_outer, 128))
    as_f32 = lambda a: jnp.asarray(np.ascontiguousarray(a), dtype=F32)
    return (as_f32(wc).astype(BF16), as_f32(a_mat).astype(BF16), as_f32(tw_cos), as_f32(tw_sin),
            as_f32(c128).astype(BF16), as_f32(s128).astype(BF16))


def _tile(seq, want):
    return min(seq, want)


def _trunk(x_groups, c_groups, w_mod, b_mod, pre_g, post_g, ffn_w_gate, ffn_w_up, ffn_w_down,
           w_in, attn_sink, fourier_w, branch_g, w_out):
    seq, d = x_groups[0].shape[1:]
    sizes = [x.shape[0] for x in x_groups]
    batch = sum(sizes)
    depth = w_mod.shape[0]
    assert seq % BLOCK == 0 and batch <= MOD_ROWS and all(x.shape[1:] == (seq, d) for x in x_groups)
    n_outer = seq // BLOCK

    c_pad = jnp.zeros((MOD_ROWS, d), F32).at[:batch].set(jnp.concatenate(c_groups, axis=0))
    mod = _modulation(c_pad, w_mod, b_mod).reshape(depth, MOD_ROWS, N_SUBLAYERS, N_MOD, d)

    cos_t, sin_t = _rope_tables(seq)
    wc, a_mat, tw_cos, tw_sin, c128, s128 = _dft_tables(seq)

    wg = ffn_w_gate.astype(BF16)
    wu = ffn_w_up.astype(BF16)
    wd = ffn_w_down.astype(BF16)
    w_in_b = w_in.astype(BF16)
    w_out_b = w_out.astype(BF16)
    w_lin = fourier_w.astype(BF16)

    tm_ffn = _tile(seq, 1024)
    tf = 512 if wg.shape[-1] % 512 == 0 else wg.shape[-1]
    tm_proj = _tile(seq, 512)
    q_blocks = min(8, n_outer)
    tiles_per_seq = seq // tm_ffn
    group_tile0 = [sum(sizes[:i]) * tiles_per_seq for i in range(len(sizes))]

    def ffn(x, l, sub, which, **kw):
        return _ffn(x, mod[l, :, sub], pre_g[l, sub], post_g[l, sub], wg, wu, wd, layer=l, which=which,
                    seq=seq, weight=0.5, tm=tm_ffn, tf=tf, **kw)

    xs = None
    for l in range(depth):
        if l == 0:
            x_parts = [ffn(x.reshape(n * seq, d), l, 0, 0, seq_tile0=t0)
                       for x, n, t0 in zip(x_groups, sizes, group_tile0)]
        else:
            x_parts = [ffn(xs, l, 0, 0)]
        q, k, v, pq = _inproj(x_parts, mod[l, :, 1], pre_g[l, 1], w_in_b, cos_t, sin_t, wc,
                              layer=l, seq=seq, tm=tm_proj)
        a_out = _attention(q, k, v, attn_sink[l], branch_g[l, 0], seq=seq, q_blocks=q_blocks)
        f_raw = _fourier_mix(pq, a_mat, tw_cos, tw_sin, c128, s128, w_lin, layer=l, batch=batch, seq=seq)
        xs = _outproj(a_out, f_raw, x_parts, mod[l, :, 1], post_g[l, 1], branch_g[l, 1], w_out_b,
                      layer=l, seq=seq, tm=tm_proj)
        if l < depth - 1:
            xs = ffn(xs, l, 2, 1)
    outs = []
    for n, t0 in zip(sizes, group_tile0):
        y = ffn(xs, depth - 1, 2, 1, tiles=n * tiles_per_seq, in_tile0=t0, seq_tile0=t0)
        outs.append(y.reshape(n, seq, d))
    return outs


def kernel(x_prompt, x_sample, c_prompt, c_sample, w_mod, b_mod, pre_g, post_g, ffn_w_gate, ffn_w_up,
           ffn_w_down, w_in, attn_sink, fourier_w, branch_g, w_out):
    y_prompt, y_sample = _trunk([x_prompt, x_sample], [c_prompt, c_sample], w_mod, b_mod, pre_g, post_g,
                                ffn_w_gate, ffn_w_up, ffn_w_down, w_in, attn_sink, fourier_w, branch_g, w_out)
    return y_prompt, y_sample
```

```python
import functools

import numpy as np
import jax
import jax.numpy as jnp
from jax import lax
from jax.experimental import pallas as pl
from jax.experimental.pallas import tpu as pltpu

HEAD_DIM = 128
N_Q_HEADS = 8
N_KV_HEADS = 2
Q_PER_KV = N_Q_HEADS // N_KV_HEADS
ATTN_WIDTH = N_Q_HEADS * HEAD_DIM
KV_WIDTH = N_KV_HEADS * HEAD_DIM
N_FOURIER_GROUPS = 8
FOURIER_GROUP_DIM = 128
FOURIER_WIDTH = N_FOURIER_GROUPS * FOURIER_GROUP_DIM
WINDOW = 128
BLOCK = 128
ROPE_THETA = 10000.0
N_SUBLAYERS = 3
N_MOD = 3
RMS_EPS = 1e-6
NEG_INF = -1e30
LOG2_E = 1.4426950408889634
MOD_ROWS = 8

V7X_VMEM_LIMIT_BYTES = 60 * 1024 * 1024

BF16 = jnp.bfloat16
F32 = jnp.float32


def _sigmoid(x):
    return 1.0 / (1.0 + jnp.exp(-x))


LANES = 128
ROW_CHUNK = 64


def _row_rms_scale(ref, r0, rows):
    d = ref.shape[-1]
    acc = None
    for c0 in range(0, d, LANES):
        t = ref[r0:r0 + rows, c0:c0 + LANES]
        acc = t * t if acc is None else acc + t * t
    return lax.rsqrt(jnp.sum(acc, axis=-1, keepdims=True) * (1.0 / d) + RMS_EPS)


def _pre_into(h_ref, x_ref, gain_ref, mod_ref, mult_ref):
    rows, d = x_ref.shape
    rc = min(ROW_CHUNK, rows)
    mult_ref[...] = gain_ref[...] * (1.0 + mod_ref[0, 1:2, :])
    for r0 in range(0, rows, rc):
        inv = _row_rms_scale(x_ref, r0, rc)
        for c0 in range(0, d, LANES):
            sl = slice(c0, c0 + LANES)
            t = (x_ref[r0:r0 + rc, sl] * inv) * mult_ref[:, sl] + mod_ref[0, 0:1, sl]
            h_ref[r0:r0 + rc, sl] = t.astype(h_ref.dtype)


def _post_into(o_ref, x_ref, gain_ref, mod_ref, mult_ref, weight):
    rows, d = x_ref.shape
    rc = min(ROW_CHUNK, rows)
    mult_ref[...] = (weight * (1.0 + mod_ref[0, 2:3, :])) * gain_ref[...]
    for r0 in range(0, rows, rc):
        inv = _row_rms_scale(o_ref, r0, rc)
        for c0 in range(0, d, LANES):
            sl = slice(c0, c0 + LANES)
            y = o_ref[r0:r0 + rc, sl]
            o_ref[r0:r0 + rc, sl] = x_ref[r0:r0 + rc, sl] + (y * inv) * mult_ref[:, sl]


def _part_specs(parts, tm):
    specs, ranges, t0 = [], [], 0
    for p in parts:
        n = p.shape[0] // tm
        specs.append(pl.BlockSpec((tm, p.shape[1]), lambda r, t0=t0, n=n: (jnp.clip(r - t0, 0, n - 1), 0)))
        ranges.append((t0, t0 + n))
        t0 += n
    return specs, ranges


def _with_owner(refs, ranges, fn):
    if len(refs) == 1:
        fn(refs[0])
        return
    r = pl.program_id(0)
    for ref, (lo, hi) in zip(refs, ranges):
        pl.when((r >= lo) & (r < hi))(functools.partial(fn, ref))


def _mod_kernel(c_ref, w_ref, b_ref, o_ref):
    c = c_ref[...]
    act = (c * _sigmoid(c)).astype(BF16)
    o_ref[0] = jnp.dot(act, w_ref[0].astype(BF16), preferred_element_type=F32) + b_ref[0]


def _modulation(c_pad, w_mod, b_mod):
    depth, d, width = w_mod.shape
    tn = 1024
    return pl.pallas_call(
        _mod_kernel,
        grid=(depth, width // tn),
        in_specs=[
            pl.BlockSpec((MOD_ROWS, d), lambda l, n: (0, 0)),
            pl.BlockSpec((1, d, tn), lambda l, n: (l, 0, n)),
            pl.BlockSpec((1, 1, tn), lambda l, n: (l, 0, n)),
        ],
        out_specs=pl.BlockSpec((1, MOD_ROWS, tn), lambda l, n: (l, 0, n)),
        out_shape=jax.ShapeDtypeStruct((depth, MOD_ROWS, width), F32),
        compiler_params=pltpu.CompilerParams(dimension_semantics=("parallel", "parallel")),
        name="modulation",
    )(c_pad, w_mod, b_mod.reshape(depth, 1, width))


def _ffn_kernel(x_ref, mod_ref, pre_g_ref, post_g_ref, wg_ref, wu_ref, wd_ref, o_ref, h_ref, mult_ref,
                *, n_chunks, weight):
    j = pl.program_id(1)

    def down_projection():
        h = h_ref[...]
        tf = wg_ref.shape[1]
        halves = [slice(c0, c0 + tf // 2) for c0 in (0, tf // 2)]
        gu = [(jnp.dot(h, wg_ref[:, sl], preferred_element_type=F32),
               jnp.dot(h, wu_ref[:, sl], preferred_element_type=F32)) for sl in halves]
        down = None
        for sl, (g, u) in zip(halves, gu):
            a = ((g * _sigmoid(g)) * u).astype(BF16)
            part = jnp.dot(a, wd_ref[sl, :], preferred_element_type=F32)
            down = part if down is None else down + part
        return down

    @pl.when(j == 0)
    def _():
        _pre_into(h_ref, x_ref, pre_g_ref, mod_ref, mult_ref)
        o_ref[...] = down_projection()

    @pl.when(j > 0)
    def _():
        o_ref[...] += down_projection()

    @pl.when(j == n_chunks - 1)
    def _():
        _post_into(o_ref, x_ref, post_g_ref, mod_ref, mult_ref, weight)


def _ffn(x, mod, pre_g, post_g, wg, wu, wd, *, layer, which, seq, weight, tm, tf,
         tiles=None, in_tile0=0, seq_tile0=0):
    d = x.shape[1]
    d_ff = wg.shape[-1]
    tiles = x.shape[0] // tm if tiles is None else tiles
    tiles_per_seq = seq // tm
    n_chunks = d_ff // tf
    return pl.pallas_call(
        functools.partial(_ffn_kernel, n_chunks=n_chunks, weight=weight),
        grid=(tiles, n_chunks),
        in_specs=[
            pl.BlockSpec((tm, d), lambda r, j: (r + in_tile0, 0)),
            pl.BlockSpec((1, N_MOD, d), lambda r, j: ((r + seq_tile0) // tiles_per_seq, 0, 0)),
            pl.BlockSpec((1, d), lambda r, j: (0, 0)),
            pl.BlockSpec((1, d), lambda r, j: (0, 0)),
            pl.BlockSpec((None, None, d, tf), lambda r, j: (layer, which, 0, j)),
            pl.BlockSpec((None, None, d, tf), lambda r, j: (layer, which, 0, j)),
            pl.BlockSpec((None, None, tf, d), lambda r, j: (layer, which, j, 0)),
        ],
        out_specs=pl.BlockSpec((tm, d), lambda r, j: (r, 0)),
        out_shape=jax.ShapeDtypeStruct((tiles * tm, d), F32),
        scratch_shapes=[pltpu.VMEM((tm, d), BF16), pltpu.VMEM((1, d), F32)],
        compiler_params=pltpu.CompilerParams(
            dimension_semantics=("parallel", "arbitrary"),
            vmem_limit_bytes=V7X_VMEM_LIMIT_BYTES),
        name="ffn",
    )(x, mod, pre_g.reshape(1, d), post_g.reshape(1, d), wg, wu, wd)


def _inproj_kernel(*refs, x_ranges):
    x_refs = refs[:len(x_ranges)]
    (mod_ref, pre_g_ref, w_ref, cos_ref, sin_ref, wc_ref,
     q_ref, k_ref, v_ref, pq_ref, h_ref, mult_ref) = refs[len(x_ranges):]

    def body(x_ref):
        _pre_into(h_ref, x_ref, pre_g_ref, mod_ref, mult_ref)
        proj = jnp.dot(h_ref[...], w_ref[...], preferred_element_type=F32)
        cos_t = cos_ref[...]
        sin_t = sin_ref[...]

        def rope(t):
            return t * cos_t + pltpu.roll(t, HEAD_DIM // 2, 1) * sin_t

        q_scale = LOG2_E * HEAD_DIM ** -0.5
        for hh in range(N_Q_HEADS):
            sl = slice(hh * HEAD_DIM, (hh + 1) * HEAD_DIM)
            q_ref[:, sl] = (rope(proj[:, sl]) * q_scale).astype(BF16)
        for hh in range(N_KV_HEADS):
            src = slice(ATTN_WIDTH + hh * HEAD_DIM, ATTN_WIDTH + (hh + 1) * HEAD_DIM)
            k_ref[:, hh * HEAD_DIM:(hh + 1) * HEAD_DIM] = rope(proj[:, src]).astype(BF16)
        v_ref[...] = proj[:, ATTN_WIDTH + KV_WIDTH:ATTN_WIDTH + 2 * KV_WIDTH].astype(BF16)
        u0 = ATTN_WIDTH + 2 * KV_WIDTH
        wc = wc_ref[...]
        for g in range(N_FOURIER_GROUPS):
            dst = slice(g * FOURIER_GROUP_DIM, (g + 1) * FOURIER_GROUP_DIM)
            ug = proj[:, u0 + g * FOURIER_GROUP_DIM:u0 + (g + 1) * FOURIER_GROUP_DIM].astype(BF16)
            pq = jnp.dot(ug, wc, preferred_element_type=F32)
            pq_ref[0, :, dst] = pq[:, :FOURIER_GROUP_DIM].astype(BF16)
            pq_ref[1, :, dst] = pq[:, FOURIER_GROUP_DIM:].astype(BF16)

    _with_owner(x_refs, x_ranges, body)


def _inproj(x_parts, mod, pre_g, w_in, cos_t, sin_t, wc, *, layer, seq, tm):
    rows, d = sum(p.shape[0] for p in x_parts), x_parts[0].shape[1]
    tiles_per_seq = seq // tm
    width = w_in.shape[-1]
    x_specs, x_ranges = _part_specs(x_parts, tm)
    return pl.pallas_call(
        functools.partial(_inproj_kernel, x_ranges=x_ranges),
        grid=(rows // tm,),
        in_specs=x_specs + [
            pl.BlockSpec((1, N_MOD, d), lambda r: (r // tiles_per_seq, 0, 0)),
            pl.BlockSpec((1, d), lambda r: (0, 0)),
            pl.BlockSpec((None, d, width), lambda r: (layer, 0, 0)),
            pl.BlockSpec((tm, HEAD_DIM), lambda r: (r % tiles_per_seq, 0)),
            pl.BlockSpec((tm, HEAD_DIM), lambda r: (r % tiles_per_seq, 0)),
            pl.BlockSpec((FOURIER_GROUP_DIM, 2 * FOURIER_GROUP_DIM), lambda r: (0, 0)),
        ],
        out_specs=[
            pl.BlockSpec((tm, ATTN_WIDTH), lambda r: (r, 0)),
            pl.BlockSpec((tm, KV_WIDTH), lambda r: (r, 0)),
            pl.BlockSpec((tm, KV_WIDTH), lambda r: (r, 0)),
            pl.BlockSpec((2, tm, FOURIER_WIDTH), lambda r: (0, r, 0)),
        ],
        out_shape=[
            jax.ShapeDtypeStruct((rows, ATTN_WIDTH), BF16),
            jax.ShapeDtypeStruct((rows, KV_WIDTH), BF16),
            jax.ShapeDtypeStruct((rows, KV_WIDTH), BF16),
            jax.ShapeDtypeStruct((2, rows, FOURIER_WIDTH), BF16),
        ],
        scratch_shapes=[pltpu.VMEM((tm, d), BF16), pltpu.VMEM((1, d), F32)],
        compiler_params=pltpu.CompilerParams(
            dimension_semantics=("parallel",), vmem_limit_bytes=V7X_VMEM_LIMIT_BYTES),
        name="inproj",
    )(*x_parts, mod, pre_g.reshape(1, d), w_in, cos_t, sin_t, wc)


def _attn_kernel(sink_ref, q_ref, kp_ref, kc_ref, kn_ref, vp_ref, vc_ref, vn_ref, g_ref,
                 o_ref, kbuf, vbuf, *, q_blocks, blocks_per_seq):
    tq = q_blocks * BLOCK
    band = 3 * BLOCK
    ext = 2 * HEAD_DIM
    kbuf[0:BLOCK] = kp_ref[...]
    kbuf[BLOCK:BLOCK + tq] = kc_ref[...]
    kbuf[BLOCK + tq:2 * BLOCK + tq] = kn_ref[...]
    for hk in range(N_KV_HEADS):
        src = slice(hk * HEAD_DIM, (hk + 1) * HEAD_DIM)
        dst = slice(hk * ext, hk * ext + HEAD_DIM)
        vbuf[0:BLOCK, dst] = vp_ref[:, src]
        vbuf[BLOCK:BLOCK + tq, dst] = vc_ref[:, src]
        vbuf[BLOCK + tq:2 * BLOCK + tq, dst] = vn_ref[:, src]
        vbuf[:, hk * ext + HEAD_DIM:(hk + 1) * ext] = jnp.ones((tq + 2 * BLOCK, HEAD_DIM), BF16)

    first_block = (pl.program_id(0) * q_blocks) % blocks_per_seq
    qi = lax.broadcasted_iota(jnp.int32, (BLOCK, BLOCK), 0)
    kj = lax.broadcasted_iota(jnp.int32, (BLOCK, BLOCK), 1)
    tri_prev = jnp.where(kj >= qi, 0.0, NEG_INF)
    tri_next = jnp.where(kj <= qi, 0.0, NEG_INF)
    gain = g_ref[...]
    n_chains = q_blocks * N_KV_HEADS

    def scores(c):
        b, hk = divmod(c, N_KV_HEADS)
        r0 = b * BLOCK
        qs = jnp.concatenate(
            [q_ref[r0:r0 + BLOCK, (hk * Q_PER_KV + g) * HEAD_DIM:(hk * Q_PER_KV + g + 1) * HEAD_DIM]
             for g in range(Q_PER_KV)], axis=0)
        kb = kbuf[r0:r0 + band, hk * HEAD_DIM:(hk + 1) * HEAD_DIM]
        return lax.dot_general(qs, kb, (((1,), (1,)), ((), ())), preferred_element_type=F32)

    def softmax(c, s):
        b, hk = divmod(c, N_KV_HEADS)
        n = first_block + b
        bias_prev = tri_prev + jnp.where(n == 0, NEG_INF, 0.0)
        bias_next = tri_next + jnp.where(n == blocks_per_seq - 1, NEG_INF, 0.0)
        ps, sink_terms = [], []
        for g in range(Q_PER_KV):
            sg = s[g * BLOCK:(g + 1) * BLOCK]
            s0 = sg[:, 0:BLOCK] + bias_prev
            s1 = sg[:, BLOCK:2 * BLOCK]
            s2 = sg[:, 2 * BLOCK:] + bias_next
            sink = sink_ref[hk * Q_PER_KV + g] * LOG2_E
            m = jnp.maximum(jnp.max(jnp.maximum(jnp.maximum(s0, s1), s2), axis=-1, keepdims=True), sink)
            ps.append(jnp.concatenate([jnp.exp2(s0 - m), jnp.exp2(s1 - m), jnp.exp2(s2 - m)],
                                      axis=1).astype(BF16))
            sink_terms.append(jnp.exp2(sink - m))
        return jnp.concatenate(ps, axis=0), sink_terms

    def weighted_values(c, p, sink_terms):
        b, hk = divmod(c, N_KV_HEADS)
        r0 = b * BLOCK
        oe = jnp.dot(p, vbuf[r0:r0 + band, hk * ext:(hk + 1) * ext], preferred_element_type=F32)
        outs = []
        for g in range(Q_PER_KV):
            og = oe[g * BLOCK:(g + 1) * BLOCK]
            denom = og[:, HEAD_DIM:] + sink_terms[g]
            outs.append(og[:, :HEAD_DIM] * (1.0 / denom))
        return outs

    def finish_block(b, heads):
        r0 = b * BLOCK
        sq = heads[0] * heads[0]
        for t in heads[1:]:
            sq = sq + t * t
        inv = lax.rsqrt(jnp.sum(sq, axis=-1, keepdims=True) * (1.0 / ATTN_WIDTH) + RMS_EPS)
        for hh, t in enumerate(heads):
            sl = slice(hh * HEAD_DIM, (hh + 1) * HEAD_DIM)
            o_ref[r0:r0 + BLOCK, sl] = (t * inv * gain[:, sl]).astype(BF16)

    lead = 2
    s_vals = {c: scores(c) for c in range(min(lead, n_chains))}
    p_vals, heads = {}, {}
    for c in range(n_chains):
        if c + lead < n_chains:
            s_vals[c + lead] = scores(c + lead)
        p_vals[c] = softmax(c, s_vals.pop(c))
        for done in ([c - 1] if c >= 1 else []) + ([c] if c == n_chains - 1 else []):
            b, hk = divmod(done, N_KV_HEADS)
            heads.setdefault(b, []).extend(weighted_values(done, *p_vals.pop(done)))
            if hk == N_KV_HEADS - 1:
                finish_block(b, heads.pop(b))


def _attention(q, k, v, sink, gain, *, seq, q_blocks):
    rows = q.shape[0]
    tq = q_blocks * BLOCK
    bps = seq // BLOCK

    def prev_map(r):
        g0 = r * q_blocks
        return (jnp.where(g0 % bps == 0, g0, g0 - 1), 0)

    def next_map(r):
        g1 = (r + 1) * q_blocks
        return (jnp.where(g1 % bps == 0, g1 - 1, g1), 0)

    edge = pl.BlockSpec((BLOCK, KV_WIDTH), prev_map)
    edge_n = pl.BlockSpec((BLOCK, KV_WIDTH), next_map)
    cur = pl.BlockSpec((tq, KV_WIDTH), lambda r: (r, 0))
    return pl.pallas_call(
        functools.partial(_attn_kernel, q_blocks=q_blocks, blocks_per_seq=bps),
        grid=(rows // tq,),
        in_specs=[
            pl.BlockSpec(memory_space=pltpu.SMEM),
            pl.BlockSpec((tq, ATTN_WIDTH), lambda r: (r, 0)),
            edge, cur, edge_n, edge, cur, edge_n,
            pl.BlockSpec((1, ATTN_WIDTH), lambda r: (0, 0)),
        ],
        out_specs=pl.BlockSpec((tq, ATTN_WIDTH), lambda r: (r, 0)),
        out_shape=jax.ShapeDtypeStruct((rows, ATTN_WIDTH), BF16),
        scratch_shapes=[pltpu.VMEM((tq + 2 * BLOCK, KV_WIDTH), BF16),
                        pltpu.VMEM((tq + 2 * BLOCK, 2 * KV_WIDTH), BF16)],
        compiler_params=pltpu.CompilerParams(dimension_semantics=("parallel",)),
        name="attention",
    )(sink, q, k, k, k, v, v, v, gain.reshape(1, ATTN_WIDTH))


FFT_COLS = 16


SUBLANES = 8


def _pitch(rows):
    return rows + SUBLANES if (rows // SUBLANES) % 2 == 0 else rows


def _fft_kernel(pq_ref, a_ref, tc_ref, ts_ref, c_ref, s_ref, wl_ref, o_ref, p_s, q_s, z_s, *, n_outer):
    in_pitch = _pitch(BLOCK)
    out_pitch = _pitch(n_outer)
    for n1 in range(n_outer):
        rows = slice(n1 * BLOCK, (n1 + 1) * BLOCK)
        p_s[n1 * in_pitch:n1 * in_pitch + BLOCK, :] = pq_ref[0, rows, :].astype(F32)
        q_s[n1 * in_pitch:n1 * in_pitch + BLOCK, :] = pq_ref[1, rows, :].astype(F32)

    def slow_rows(n2):
        return pl.ds(n2, n_outer, stride=in_pitch)

    a_mat = a_ref[...]
    for n2_0 in range(0, BLOCK, FFT_COLS):
        cols = range(n2_0, n2_0 + FFT_COLS)
        x = jnp.concatenate([jnp.concatenate([p_s[slow_rows(n2), :] for n2 in cols], axis=1),
                             jnp.concatenate([q_s[slow_rows(n2), :] for n2 in cols], axis=1)], axis=0)
        y = jnp.dot(a_mat, x.astype(BF16), preferred_element_type=F32)
        for t, n2 in enumerate(cols):
            yr = y[:n_outer, t * LANES:(t + 1) * LANES]
            yi = y[n_outer:, t * LANES:(t + 1) * LANES]
            tc = tc_ref[n2]
            ts = ts_ref[n2]
            p_s[slow_rows(n2), :] = yr * tc + yi * ts
            q_s[slow_rows(n2), :] = yi * tc - yr * ts

    c_mat = c_ref[...]
    s_mat = s_ref[...]
    w_lin = wl_ref[...]
    step = min(FFT_COLS, n_outer)
    for k1_0 in range(0, n_outer, step):
        ks = range(k1_0, k1_0 + step)
        yr = jnp.concatenate([p_s[k1 * in_pitch:k1 * in_pitch + BLOCK, :] for k1 in ks], axis=1).astype(BF16)
        yi = jnp.concatenate([q_s[k1 * in_pitch:k1 * in_pitch + BLOCK, :] for k1 in ks], axis=1).astype(BF16)
        z = (jnp.dot(c_mat, yr, preferred_element_type=F32)
             + jnp.dot(s_mat, yi, preferred_element_type=F32))
        z_rows = jnp.concatenate([z[:, t * LANES:(t + 1) * LANES] for t in range(step)], axis=0).astype(BF16)
        out = jnp.dot(z_rows, w_lin, preferred_element_type=F32)
        for t, k1 in enumerate(ks):
            z_s[pl.ds(k1, BLOCK, stride=out_pitch), :] = out[t * BLOCK:(t + 1) * BLOCK]
    for k2 in range(BLOCK):
        o_ref[k2 * n_outer:(k2 + 1) * n_outer, :] = z_s[k2 * out_pitch:k2 * out_pitch + n_outer, :].astype(o_ref.dtype)


def _fourier_mix(pq, a_mat, tw_cos, tw_sin, c_mat, s_mat, w_lin, *, layer, batch, seq):
    n_outer = seq // BLOCK
    const = functools.partial(pl.BlockSpec, pipeline_mode=pl.Buffered(1))
    return pl.pallas_call(
        functools.partial(_fft_kernel, n_outer=n_outer),
        grid=(batch, N_FOURIER_GROUPS),
        in_specs=[
            pl.BlockSpec((2, seq, LANES), lambda b, g: (0, b, g)),
            const((2 * n_outer, 2 * n_outer), lambda b, g: (0, 0)),
            const((BLOCK, n_outer, LANES), lambda b, g: (0, 0, 0)),
            const((BLOCK, n_outer, LANES), lambda b, g: (0, 0, 0)),
            const((BLOCK, BLOCK), lambda b, g: (0, 0)),
            const((BLOCK, BLOCK), lambda b, g: (0, 0)),
            pl.BlockSpec((None, None, FOURIER_GROUP_DIM, FOURIER_GROUP_DIM), lambda b, g: (layer, g, 0, 0)),
        ],
        out_specs=pl.BlockSpec((seq, LANES), lambda b, g: (b, g)),
        out_shape=jax.ShapeDtypeStruct((batch * seq, FOURIER_WIDTH), BF16),
        scratch_shapes=[pltpu.VMEM((n_outer * _pitch(BLOCK), LANES), F32),
                        pltpu.VMEM((n_outer * _pitch(BLOCK), LANES), F32),
                        pltpu.VMEM((BLOCK * _pitch(n_outer), LANES), F32)],
        compiler_params=pltpu.CompilerParams(
            dimension_semantics=("parallel", "parallel"), vmem_limit_bytes=V7X_VMEM_LIMIT_BYTES),
        name="fourier_mix",
    )(pq, a_mat, tw_cos, tw_sin, c_mat, s_mat, w_lin)


def _outproj_kernel(*refs, x_ranges):
    x_refs = refs[:len(x_ranges)]
    a_ref, f_ref, mod_ref, post_g_ref, fg_ref, w_ref, o_ref, fn_ref, mult_ref = refs[len(x_ranges):]

    def body(x_ref):
        rows = f_ref.shape[0]
        rc = min(ROW_CHUNK, rows)
        for r0 in range(0, rows, rc):
            acc = None
            for c0 in range(0, FOURIER_WIDTH, LANES):
                t = f_ref[r0:r0 + rc, c0:c0 + LANES].astype(F32)
                acc = t * t if acc is None else acc + t * t
            inv = lax.rsqrt(jnp.sum(acc, axis=-1, keepdims=True) * (1.0 / FOURIER_WIDTH) + RMS_EPS)
            for c0 in range(0, FOURIER_WIDTH, LANES):
                sl = slice(c0, c0 + LANES)
                fn_ref[r0:r0 + rc, sl] = (f_ref[r0:r0 + rc, sl].astype(F32) * inv * fg_ref[:, sl]).astype(BF16)
        o_ref[...] = (jnp.dot(a_ref[...], w_ref[:ATTN_WIDTH, :], preferred_element_type=F32)
                      + jnp.dot(fn_ref[...], w_ref[ATTN_WIDTH:, :], preferred_element_type=F32))
        _post_into(o_ref, x_ref, post_g_ref, mod_ref, mult_ref, 1.0)

    _with_owner(x_refs, x_ranges, body)


def _outproj(a, f, x_parts, mod, post_g, f_gain, w_out, *, layer, seq, tm):
    rows, d = sum(p.shape[0] for p in x_parts), x_parts[0].shape[1]
    tiles_per_seq = seq // tm
    x_specs, x_ranges = _part_specs(x_parts, tm)
    return pl.pallas_call(
        functools.partial(_outproj_kernel, x_ranges=x_ranges),
        grid=(rows // tm,),
        in_specs=x_specs + [
            pl.BlockSpec((tm, ATTN_WIDTH), lambda r: (r, 0)),
            pl.BlockSpec((tm, FOURIER_WIDTH), lambda r: (r, 0)),
            pl.BlockSpec((1, N_MOD, d), lambda r: (r // tiles_per_seq, 0, 0)),
            pl.BlockSpec((1, d), lambda r: (0, 0)),
            pl.BlockSpec((1, FOURIER_WIDTH), lambda r: (0, 0)),
            pl.BlockSpec((None, ATTN_WIDTH + FOURIER_WIDTH, d), lambda r: (layer, 0, 0)),
        ],
        out_specs=pl.BlockSpec((tm, d), lambda r: (r, 0)),
        out_shape=jax.ShapeDtypeStruct((rows, d), F32),
        scratch_shapes=[pltpu.VMEM((tm, FOURIER_WIDTH), BF16), pltpu.VMEM((1, d), F32)],
        compiler_params=pltpu.CompilerParams(
            dimension_semantics=("parallel",), vmem_limit_bytes=V7X_VMEM_LIMIT_BYTES),
        name="outproj",
    )(*x_parts, a, f, mod, post_g.reshape(1, d), f_gain.reshape(1, FOURIER_WIDTH), w_out)


def _rope_tables(seq):
    inv_freq = ROPE_THETA ** (-jnp.arange(0, HEAD_DIM, 2, dtype=F32) / HEAD_DIM)
    ang = jnp.arange(seq, dtype=F32)[:, None] * inv_freq[None, :]
    cos, sin = jnp.cos(ang), jnp.sin(ang)
    return jnp.concatenate([cos, cos], axis=-1), jnp.concatenate([-sin, sin], axis=-1)


def _dft_tables(seq):
    n_outer = seq // BLOCK

    def cs(n, scale):
        idx = np.arange(n)
        ang = 2.0 * np.pi * ((idx[:, None] * idx[None, :]) % n) / n
        return np.cos(ang) * scale, np.sin(ang) * scale

    cc, sc = cs(FOURIER_GROUP_DIM, FOURIER_GROUP_DIM ** -0.5)
    wc = np.concatenate([cc, sc], axis=1)
    co, so = cs(n_outer, n_outer ** -0.5)
    a_mat = np.block([[co, -so], [-so, -co]])
    c128, s128 = cs(BLOCK, BLOCK ** -0.5)
    n2 = np.arange(BLOCK)[:, None]
    k1 = np.arange(n_outer)[None, :]
    tw = 2.0 * np.pi * ((n2 * k1) % seq) / seq
    tw_cos = np.broadcast_to(np.cos(tw)[:, :, None], (BLOCK, n_outer, 128))
    tw_sin = np.broadcast_to(np.sin(tw)[:, :, None], (BLOCK, n_outer, 128))
    as_f32 = lambda a: jnp.asarray(np.ascontiguousarray(a), dtype=F32)
    return (as_f32(wc).astype(BF16), as_f32(a_mat).astype(BF16), as_f32(tw_cos), as_f32(tw_sin),
            as_f32(c128).astype(BF16), as_f32(s128).astype(BF16))


def _tile(seq, want):
    return min(seq, want)


def _trunk(x_groups, c_groups, w_mod, b_mod, pre_g, post_g, ffn_w_gate, ffn_w_up, ffn_w_down,
           w_in, attn_sink, fourier_w, branch_g, w_out):
    seq, d = x_groups[0].shape[1:]
    sizes = [x.shape[0] for x in x_groups]
    batch = sum(sizes)
    depth = w_mod.shape[0]
    assert seq % BLOCK == 0 and batch <= MOD_ROWS and all(x.shape[1:] == (seq, d) for x in x_groups)
    n_outer = seq // BLOCK

    c_pad = jnp.zeros((MOD_ROWS, d), F32).at[:batch].set(jnp.concatenate(c_groups, axis=0))
    mod = _modulation(c_pad, w_mod, b_mod).reshape(depth, MOD_ROWS, N_SUBLAYERS, N_MOD, d)

    cos_t, sin_t = _rope_tables(seq)
    wc, a_mat, tw_cos, tw_sin, c128, s128 = _dft_tables(seq)

    wg = ffn_w_gate.astype(BF16)
    wu = ffn_w_up.astype(BF16)
    wd = ffn_w_down.astype(BF16)
    w_in_b = w_in.astype(BF16)
    w_out_b = w_out.astype(BF16)
    w_lin = fourier_w.astype(BF16)

    tm_ffn = _tile(seq, 1024)
    tf = 512 if wg.shape[-1] % 512 == 0 else wg.shape[-1]
    tm_proj = _tile(seq, 512)
    q_blocks = min(16, n_outer)
    tiles_per_seq = seq // tm_ffn
    group_tile0 = [sum(sizes[:i]) * tiles_per_seq for i in range(len(sizes))]

    def ffn(x, l, sub, which, **kw):
        return _ffn(x, mod[l, :, sub], pre_g[l, sub], post_g[l, sub], wg, wu, wd, layer=l, which=which,
                    seq=seq, weight=0.5, tm=tm_ffn, tf=tf, **kw)

    xs = None
    for l in range(depth):
        if l == 0:
            x_parts = [ffn(x.reshape(n * seq, d), l, 0, 0, seq_tile0=t0)
                       for x, n, t0 in zip(x_groups, sizes, group_tile0)]
        else:
            x_parts = [ffn(xs, l, 0, 0)]
        q, k, v, pq = _inproj(x_parts, mod[l, :, 1], pre_g[l, 1], w_in_b, cos_t, sin_t, wc,
                              layer=l, seq=seq, tm=tm_proj)
        a_out = _attention(q, k, v, attn_sink[l], branch_g[l, 0], seq=seq, q_blocks=q_blocks)
        f_raw = _fourier_mix(pq, a_mat, tw_cos, tw_sin, c128, s128, w_lin, layer=l, batch=batch, seq=seq)
        xs = _outproj(a_out, f_raw, x_parts, mod[l, :, 1], post_g[l, 1], branch_g[l, 1], w_out_b,
                      layer=l, seq=seq, tm=tm_proj)
        if l < depth - 1:
            xs = ffn(xs, l, 2, 1)
    outs = []
    for n, t0 in zip(sizes, group_tile0):
        y = ffn(xs, depth - 1, 2, 1, tiles=n * tiles_per_seq, in_tile0=t0, seq_tile0=t0)
        outs.append(y.reshape(n, seq, d))
    return outs


def kernel(x_prompt, x_sample, c_prompt, c_sample, w_mod, b_mod, pre_g, post_g, ffn_w_gate, ffn_w_up,
           ffn_w_down, w_in, attn_sink, fourier_w, branch_g, w_out):
    y_prompt, y_sample = _trunk([x_prompt, x_sample], [c_prompt, c_sample], w_mod, b_mod, pre_g, post_g,
                                ffn_w_gate, ffn_w_up, ffn_w_down, w_in, attn_sink, fourier_w, branch_g, w_out)
    return y_prompt, y_sample
```

```python
import functools

import numpy as np
import jax
import jax.numpy as jnp
from jax import lax
from jax.experimental import pallas as pl
from jax.experimental.pallas import tpu as pltpu

HEAD_DIM = 128
N_Q_HEADS = 8
N_KV_HEADS = 2
Q_PER_KV = N_Q_HEADS // N_KV_HEADS
ATTN_WIDTH = N_Q_HEADS * HEAD_DIM
KV_WIDTH = N_KV_HEADS * HEAD_DIM
N_FOURIER_GROUPS = 8
FOURIER_GROUP_DIM = 128
FOURIER_WIDTH = N_FOURIER_GROUPS * FOURIER_GROUP_DIM
WINDOW = 128
BLOCK = 128
ROPE_THETA = 10000.0
N_SUBLAYERS = 3
N_MOD = 3
RMS_EPS = 1e-6
NEG_INF = -1e30
LOG2_E = 1.4426950408889634
MOD_ROWS = 8

V7X_VMEM_LIMIT_BYTES = 60 * 1024 * 1024

BF16 = jnp.bfloat16
F32 = jnp.float32


def _sigmoid(x):
    return 1.0 / (1.0 + jnp.exp(-x))


LANES = 128
ROW_CHUNK = 64


def _row_rms_scale(ref, r0, rows):
    d = ref.shape[-1]
    acc = None
    for c0 in range(0, d, LANES):
        t = ref[r0:r0 + rows, c0:c0 + LANES]
        acc = t * t if acc is None else acc + t * t
    return lax.rsqrt(jnp.sum(acc, axis=-1, keepdims=True) * (1.0 / d) + RMS_EPS)


def _pre_into(h_ref, x_ref, gain_ref, mod_ref, mult_ref):
    rows, d = x_ref.shape
    rc = min(ROW_CHUNK, rows)
    mult_ref[...] = gain_ref[...] * (1.0 + mod_ref[0, 1:2, :])
    for r0 in range(0, rows, rc):
        inv = _row_rms_scale(x_ref, r0, rc)
        for c0 in range(0, d, LANES):
            sl = slice(c0, c0 + LANES)
            t = (x_ref[r0:r0 + rc, sl] * inv) * mult_ref[:, sl] + mod_ref[0, 0:1, sl]
            h_ref[r0:r0 + rc, sl] = t.astype(h_ref.dtype)


def _post_into(o_ref, x_ref, gain_ref, mod_ref, mult_ref, weight, row_range=None):
    rows, d = x_ref.shape
    rc = min(ROW_CHUNK, rows)
    lo, hi = (0, rows) if row_range is None else row_range
    mult_ref[...] = (weight * (1.0 + mod_ref[0, 2:3, :])) * gain_ref[...]
    for r0 in range(lo, hi, rc):
        inv = _row_rms_scale(o_ref, r0, rc)
        for c0 in range(0, d, LANES):
            sl = slice(c0, c0 + LANES)
            y = o_ref[r0:r0 + rc, sl]
            o_ref[r0:r0 + rc, sl] = x_ref[r0:r0 + rc, sl] + (y * inv) * mult_ref[:, sl]


def _part_specs(parts, tm):
    specs, ranges, t0 = [], [], 0
    for p in parts:
        n = p.shape[0] // tm
        specs.append(pl.BlockSpec((tm, p.shape[1]), lambda r, t0=t0, n=n: (jnp.clip(r - t0, 0, n - 1), 0)))
        ranges.append((t0, t0 + n))
        t0 += n
    return specs, ranges


def _with_owner(refs, ranges, fn):
    if len(refs) == 1:
        fn(refs[0])
        return
    r = pl.program_id(0)
    for ref, (lo, hi) in zip(refs, ranges):
        pl.when((r >= lo) & (r < hi))(functools.partial(fn, ref))


def _mod_kernel(c_ref, w_ref, b_ref, o_ref):
    c = c_ref[...]
    act = (c * _sigmoid(c)).astype(BF16)
    o_ref[0] = jnp.dot(act, w_ref[0].astype(BF16), preferred_element_type=F32) + b_ref[0]


def _modulation(c_pad, w_mod, b_mod):
    depth, d, width = w_mod.shape
    tn = 1024
    return pl.pallas_call(
        _mod_kernel,
        grid=(depth, width // tn),
        in_specs=[
            pl.BlockSpec((MOD_ROWS, d), lambda l, n: (0, 0)),
            pl.BlockSpec((1, d, tn), lambda l, n: (l, 0, n)),
            pl.BlockSpec((1, 1, tn), lambda l, n: (l, 0, n)),
        ],
        out_specs=pl.BlockSpec((1, MOD_ROWS, tn), lambda l, n: (l, 0, n)),
        out_shape=jax.ShapeDtypeStruct((depth, MOD_ROWS, width), F32),
        compiler_params=pltpu.CompilerParams(dimension_semantics=("parallel", "parallel")),
        name="modulation",
    )(c_pad, w_mod, b_mod.reshape(depth, 1, width))


def _ffn_kernel(x_ref, mod_ref, pre_g_ref, post_g_ref, wg_ref, wu_ref, wd_ref, o_ref, h_ref, mult_ref,
                *, n_chunks, weight):
    j = pl.program_id(1)

    def down_projection(rows=slice(None)):
        h = h_ref[rows, :]
        tf = wg_ref.shape[1]
        halves = [slice(c0, c0 + tf // 2) for c0 in (0, tf // 2)]
        gu = [(jnp.dot(h, wg_ref[:, sl], preferred_element_type=F32),
               jnp.dot(h, wu_ref[:, sl], preferred_element_type=F32)) for sl in halves]
        down = None
        for sl, (g, u) in zip(halves, gu):
            a = ((g * _sigmoid(g)) * u).astype(BF16)
            part = jnp.dot(a, wd_ref[sl, :], preferred_element_type=F32)
            down = part if down is None else down + part
        return down

    @pl.when(j == 0)
    def _():
        _pre_into(h_ref, x_ref, pre_g_ref, mod_ref, mult_ref)
        o_ref[...] = down_projection()

    @pl.when((j > 0) & (j < n_chunks - 1))
    def _():
        o_ref[...] += down_projection()

    @pl.when(j == n_chunks - 1)
    def _():
        half = o_ref.shape[0] // 2
        for lo in (0, half):
            rows = slice(lo, lo + half)
            o_ref[rows, :] += down_projection(rows)
            _post_into(o_ref, x_ref, post_g_ref, mod_ref, mult_ref, weight, (lo, lo + half))


def _ffn(x, mod, pre_g, post_g, wg, wu, wd, *, layer, which, seq, weight, tm, tf,
         tiles=None, in_tile0=0, seq_tile0=0):
    d = x.shape[1]
    d_ff = wg.shape[-1]
    tiles = x.shape[0] // tm if tiles is None else tiles
    tiles_per_seq = seq // tm
    n_chunks = d_ff // tf
    assert n_chunks >= 2
    return pl.pallas_call(
        functools.partial(_ffn_kernel, n_chunks=n_chunks, weight=weight),
        grid=(tiles, n_chunks),
        in_specs=[
            pl.BlockSpec((tm, d), lambda r, j: (r + in_tile0, 0)),
            pl.BlockSpec((1, N_MOD, d), lambda r, j: ((r + seq_tile0) // tiles_per_seq, 0, 0)),
            pl.BlockSpec((1, d), lambda r, j: (0, 0)),
            pl.BlockSpec((1, d), lambda r, j: (0, 0)),
            pl.BlockSpec((None, None, d, tf), lambda r, j: (layer, which, 0, j)),
            pl.BlockSpec((None, None, d, tf), lambda r, j: (layer, which, 0, j)),
            pl.BlockSpec((None, None, tf, d), lambda r, j: (layer, which, j, 0)),
        ],
        out_specs=pl.BlockSpec((tm, d), lambda r, j: (r, 0)),
        out_shape=jax.ShapeDtypeStruct((tiles * tm, d), F32),
        scratch_shapes=[pltpu.VMEM((tm, d), BF16), pltpu.VMEM((1, d), F32)],
        compiler_params=pltpu.CompilerParams(
            dimension_semantics=("parallel", "arbitrary"),
            vmem_limit_bytes=V7X_VMEM_LIMIT_BYTES),
        name="ffn",
    )(x, mod, pre_g.reshape(1, d), post_g.reshape(1, d), wg, wu, wd)


def _inproj_kernel(*refs, x_ranges):
    x_refs = refs[:len(x_ranges)]
    (mod_ref, pre_g_ref, w_ref, cos_ref, sin_ref, wc_ref,
     q_ref, k_ref, v_ref, pq_ref, h_ref, mult_ref) = refs[len(x_ranges):]

    def body(x_ref):
        _pre_into(h_ref, x_ref, pre_g_ref, mod_ref, mult_ref)
        proj = jnp.dot(h_ref[...], w_ref[...], preferred_element_type=F32)
        cos_t = cos_ref[...]
        sin_t = sin_ref[...]

        def rope(t):
            return t * cos_t + pltpu.roll(t, HEAD_DIM // 2, 1) * sin_t

        q_scale = LOG2_E * HEAD_DIM ** -0.5
        for hh in range(N_Q_HEADS):
            sl = slice(hh * HEAD_DIM, (hh + 1) * HEAD_DIM)
            q_ref[:, sl] = (rope(proj[:, sl]) * q_scale).astype(BF16)
        for hh in range(N_KV_HEADS):
            src = slice(ATTN_WIDTH + hh * HEAD_DIM, ATTN_WIDTH + (hh + 1) * HEAD_DIM)
            k_ref[:, hh * HEAD_DIM:(hh + 1) * HEAD_DIM] = rope(proj[:, src]).astype(BF16)
        v_ref[...] = proj[:, ATTN_WIDTH + KV_WIDTH:ATTN_WIDTH + 2 * KV_WIDTH].astype(BF16)
        u0 = ATTN_WIDTH + 2 * KV_WIDTH
        wc = wc_ref[...]
        for g in range(N_FOURIER_GROUPS):
            dst = slice(g * FOURIER_GROUP_DIM, (g + 1) * FOURIER_GROUP_DIM)
            ug = proj[:, u0 + g * FOURIER_GROUP_DIM:u0 + (g + 1) * FOURIER_GROUP_DIM].astype(BF16)
            pq = jnp.dot(ug, wc, preferred_element_type=F32)
            pq_ref[0, :, dst] = pq[:, :FOURIER_GROUP_DIM].astype(BF16)
            pq_ref[1, :, dst] = pq[:, FOURIER_GROUP_DIM:].astype(BF16)

    _with_owner(x_refs, x_ranges, body)


def _inproj(x_parts, mod, pre_g, w_in, cos_t, sin_t, wc, *, layer, seq, tm):
    rows, d = sum(p.shape[0] for p in x_parts), x_parts[0].shape[1]
    tiles_per_seq = seq // tm
    width = w_in.shape[-1]
    x_specs, x_ranges = _part_specs(x_parts, tm)
    return pl.pallas_call(
        functools.partial(_inproj_kernel, x_ranges=x_ranges),
        grid=(rows // tm,),
        in_specs=x_specs + [
            pl.BlockSpec((1, N_MOD, d), lambda r: (r // tiles_per_seq, 0, 0)),
            pl.BlockSpec((1, d), lambda r: (0, 0)),
            pl.BlockSpec((None, d, width), lambda r: (layer, 0, 0)),
            pl.BlockSpec((tm, HEAD_DIM), lambda r: (r % tiles_per_seq, 0)),
            pl.BlockSpec((tm, HEAD_DIM), lambda r: (r % tiles_per_seq, 0)),
            pl.BlockSpec((FOURIER_GROUP_DIM, 2 * FOURIER_GROUP_DIM), lambda r: (0, 0)),
        ],
        out_specs=[
            pl.BlockSpec((tm, ATTN_WIDTH), lambda r: (r, 0)),
            pl.BlockSpec((tm, KV_WIDTH), lambda r: (r, 0)),
            pl.BlockSpec((tm, KV_WIDTH), lambda r: (r, 0)),
            pl.BlockSpec((2, tm, FOURIER_WIDTH), lambda r: (0, r, 0)),
        ],
        out_shape=[
            jax.ShapeDtypeStruct((rows, ATTN_WIDTH), BF16),
            jax.ShapeDtypeStruct((rows, KV_WIDTH), BF16),
            jax.ShapeDtypeStruct((rows, KV_WIDTH), BF16),
            jax.ShapeDtypeStruct((2, rows, FOURIER_WIDTH), BF16),
        ],
        scratch_shapes=[pltpu.VMEM((tm, d), BF16), pltpu.VMEM((1, d), F32)],
        compiler_params=pltpu.CompilerParams(
            dimension_semantics=("parallel",), vmem_limit_bytes=V7X_VMEM_LIMIT_BYTES),
        name="inproj",
    )(*x_parts, mod, pre_g.reshape(1, d), w_in, cos_t, sin_t, wc)


def _attn_kernel(sink_ref, q_ref, kp_ref, kc_ref, kn_ref, vp_ref, vc_ref, vn_ref, g_ref,
                 o_ref, kbuf, vbuf, *, q_blocks, blocks_per_seq):
    tq = q_blocks * BLOCK
    band = 3 * BLOCK
    ext = 2 * HEAD_DIM
    kbuf[0:BLOCK] = kp_ref[...]
    kbuf[BLOCK:BLOCK + tq] = kc_ref[...]
    kbuf[BLOCK + tq:2 * BLOCK + tq] = kn_ref[...]
    for hk in range(N_KV_HEADS):
        src = slice(hk * HEAD_DIM, (hk + 1) * HEAD_DIM)
        dst = slice(hk * ext, hk * ext + HEAD_DIM)
        vbuf[0:BLOCK, dst] = vp_ref[:, src]
        vbuf[BLOCK:BLOCK + tq, dst] = vc_ref[:, src]
        vbuf[BLOCK + tq:2 * BLOCK + tq, dst] = vn_ref[:, src]
        vbuf[:, hk * ext + HEAD_DIM:(hk + 1) * ext] = jnp.ones((tq + 2 * BLOCK, HEAD_DIM), BF16)

    first_block = (pl.program_id(0) * q_blocks) % blocks_per_seq
    qi = lax.broadcasted_iota(jnp.int32, (BLOCK, BLOCK), 0)
    kj = lax.broadcasted_iota(jnp.int32, (BLOCK, BLOCK), 1)
    tri_prev = jnp.where(kj >= qi, 0.0, NEG_INF)
    tri_next = jnp.where(kj <= qi, 0.0, NEG_INF)
    gain = g_ref[...]
    n_chains = q_blocks * N_KV_HEADS

    def scores(c):
        b, hk = divmod(c, N_KV_HEADS)
        r0 = b * BLOCK
        qs = jnp.concatenate(
            [q_ref[r0:r0 + BLOCK, (hk * Q_PER_KV + g) * HEAD_DIM:(hk * Q_PER_KV + g + 1) * HEAD_DIM]
             for g in range(Q_PER_KV)], axis=0)
        kb = kbuf[r0:r0 + band, hk * HEAD_DIM:(hk + 1) * HEAD_DIM]
        return lax.dot_general(qs, kb, (((1,), (1,)), ((), ())), preferred_element_type=F32)

    def softmax(c, s):
        b, hk = divmod(c, N_KV_HEADS)
        n = first_block + b
        bias_prev = tri_prev + jnp.where(n == 0, NEG_INF, 0.0)
        bias_next = tri_next + jnp.where(n == blocks_per_seq - 1, NEG_INF, 0.0)
        ps, sink_terms = [], []
        for g in range(Q_PER_KV):
            sg = s[g * BLOCK:(g + 1) * BLOCK]
            s0 = sg[:, 0:BLOCK] + bias_prev
            s1 = sg[:, BLOCK:2 * BLOCK]
            s2 = sg[:, 2 * BLOCK:] + bias_next
            sink = sink_ref[hk * Q_PER_KV + g] * LOG2_E
            m = jnp.maximum(jnp.max(jnp.maximum(jnp.maximum(s0, s1), s2), axis=-1, keepdims=True), sink)
            ps.append(jnp.concatenate([jnp.exp2(s0 - m), jnp.exp2(s1 - m), jnp.exp2(s2 - m)],
                                      axis=1).astype(BF16))
            sink_terms.append(jnp.exp2(sink - m))
        return jnp.concatenate(ps, axis=0), sink_terms

    def weighted_values(c, p, sink_terms):
        b, hk = divmod(c, N_KV_HEADS)
        r0 = b * BLOCK
        oe = jnp.dot(p, vbuf[r0:r0 + band, hk * ext:(hk + 1) * ext], preferred_element_type=F32)
        outs = []
        for g in range(Q_PER_KV):
            og = oe[g * BLOCK:(g + 1) * BLOCK]
            denom = og[:, HEAD_DIM:] + sink_terms[g]
            outs.append(og[:, :HEAD_DIM] * (1.0 / denom))
        return outs

    def finish_block(b, heads):
        r0 = b * BLOCK
        sq = heads[0] * heads[0]
        for t in heads[1:]:
            sq = sq + t * t
        inv = lax.rsqrt(jnp.sum(sq, axis=-1, keepdims=True) * (1.0 / ATTN_WIDTH) + RMS_EPS)
        for hh, t in enumerate(heads):
            sl = slice(hh * HEAD_DIM, (hh + 1) * HEAD_DIM)
            o_ref[r0:r0 + BLOCK, sl] = (t * inv * gain[:, sl]).astype(BF16)

    lead = 2
    s_vals = {c: scores(c) for c in range(min(lead, n_chains))}
    p_vals, heads = {}, {}
    for c in range(n_chains):
        if c + lead < n_chains:
            s_vals[c + lead] = scores(c + lead)
        p_vals[c] = softmax(c, s_vals.pop(c))
        for done in ([c - 1] if c >= 1 else []) + ([c] if c == n_chains - 1 else []):
            b, hk = divmod(done, N_KV_HEADS)
            heads.setdefault(b, []).extend(weighted_values(done, *p_vals.pop(done)))
            if hk == N_KV_HEADS - 1:
                finish_block(b, heads.pop(b))


def _attention(q, k, v, sink, gain, *, seq, q_blocks):
    rows = q.shape[0]
    tq = q_blocks * BLOCK
    bps = seq // BLOCK

    def prev_map(r):
        g0 = r * q_blocks
        return (jnp.where(g0 % bps == 0, g0, g0 - 1), 0)

    def next_map(r):
        g1 = (r + 1) * q_blocks
        return (jnp.where(g1 % bps == 0, g1 - 1, g1), 0)

    edge = pl.BlockSpec((BLOCK, KV_WIDTH), prev_map)
    edge_n = pl.BlockSpec((BLOCK, KV_WIDTH), next_map)
    cur = pl.BlockSpec((tq, KV_WIDTH), lambda r: (r, 0))
    return pl.pallas_call(
        functools.partial(_attn_kernel, q_blocks=q_blocks, blocks_per_seq=bps),
        grid=(rows // tq,),
        in_specs=[
            pl.BlockSpec(memory_space=pltpu.SMEM),
            pl.BlockSpec((tq, ATTN_WIDTH), lambda r: (r, 0)),
            edge, cur, edge_n, edge, cur, edge_n,
            pl.BlockSpec((1, ATTN_WIDTH), lambda r: (0, 0)),
        ],
        out_specs=pl.BlockSpec((tq, ATTN_WIDTH), lambda r: (r, 0)),
        out_shape=jax.ShapeDtypeStruct((rows, ATTN_WIDTH), BF16),
        scratch_shapes=[pltpu.VMEM((tq + 2 * BLOCK, KV_WIDTH), BF16),
                        pltpu.VMEM((tq + 2 * BLOCK, 2 * KV_WIDTH), BF16)],
        compiler_params=pltpu.CompilerParams(dimension_semantics=("parallel",)),
        name="attention",
    )(sink, q, k, k, k, v, v, v, gain.reshape(1, ATTN_WIDTH))


FFT_COLS = 16


SUBLANES = 8


def _pitch(rows):
    return rows + SUBLANES if (rows // SUBLANES) % 2 == 0 else rows


def _fft_kernel(pq_ref, a_ref, tc_ref, ts_ref, c_ref, s_ref, wl_ref, o_ref, p_s, q_s, z_s, *, n_outer):
    in_pitch = _pitch(BLOCK)
    out_pitch = _pitch(n_outer)
    for n1 in range(n_outer):
        rows = slice(n1 * BLOCK, (n1 + 1) * BLOCK)
        p_s[n1 * in_pitch:n1 * in_pitch + BLOCK, :] = pq_ref[0, rows, :].astype(F32)
        q_s[n1 * in_pitch:n1 * in_pitch + BLOCK, :] = pq_ref[1, rows, :].astype(F32)

    def slow_rows(n2):
        return pl.ds(n2, n_outer, stride=in_pitch)

    a_mat = a_ref[...]
    for n2_0 in range(0, BLOCK, FFT_COLS):
        cols = range(n2_0, n2_0 + FFT_COLS)
        x = jnp.concatenate([jnp.concatenate([p_s[slow_rows(n2), :] for n2 in cols], axis=1),
                             jnp.concatenate([q_s[slow_rows(n2), :] for n2 in cols], axis=1)], axis=0)
        y = jnp.dot(a_mat, x.astype(BF16), preferred_element_type=F32)
        for t, n2 in enumerate(cols):
            yr = y[:n_outer, t * LANES:(t + 1) * LANES]
            yi = y[n_outer:, t * LANES:(t + 1) * LANES]
            tc = tc_ref[n2]
            ts = ts_ref[n2]
            p_s[slow_rows(n2), :] = yr * tc + yi * ts
            q_s[slow_rows(n2), :] = yi * tc - yr * ts

    c_mat = c_ref[...]
    s_mat = s_ref[...]
    w_lin = wl_ref[...]
    step = min(FFT_COLS, n_outer)
    for k1_0 in range(0, n_outer, step):
        ks = range(k1_0, k1_0 + step)
        yr = jnp.concatenate([p_s[k1 * in_pitch:k1 * in_pitch + BLOCK, :] for k1 in ks], axis=1).astype(BF16)
        yi = jnp.concatenate([q_s[k1 * in_pitch:k1 * in_pitch + BLOCK, :] for k1 in ks], axis=1).astype(BF16)
        z = (jnp.dot(c_mat, yr, preferred_element_type=F32)
             + jnp.dot(s_mat, yi, preferred_element_type=F32))
        z_rows = jnp.concatenate([z[:, t * LANES:(t + 1) * LANES] for t in range(step)], axis=0).astype(BF16)
        out = jnp.dot(z_rows, w_lin, preferred_element_type=F32)
        for t, k1 in enumerate(ks):
            z_s[pl.ds(k1, BLOCK, stride=out_pitch), :] = out[t * BLOCK:(t + 1) * BLOCK]
    for k2 in range(BLOCK):
        o_ref[k2 * n_outer:(k2 + 1) * n_outer, :] = z_s[k2 * out_pitch:k2 * out_pitch + n_outer, :].astype(o_ref.dtype)


def _fourier_mix(pq, a_mat, tw_cos, tw_sin, c_mat, s_mat, w_lin, *, layer, batch, seq):
    n_outer = seq // BLOCK
    const = functools.partial(pl.BlockSpec, pipeline_mode=pl.Buffered(1))
    return pl.pallas_call(
        functools.partial(_fft_kernel, n_outer=n_outer),
        grid=(batch, N_FOURIER_GROUPS),
        in_specs=[
            pl.BlockSpec((2, seq, LANES), lambda b, g: (0, b, g)),
            const((2 * n_outer, 2 * n_outer), lambda b, g: (0, 0)),
            const((BLOCK, n_outer, LANES), lambda b, g: (0, 0, 0)),
            const((BLOCK, n_outer, LANES), lambda b, g: (0, 0, 0)),
            const((BLOCK, BLOCK), lambda b, g: (0, 0)),
            const((BLOCK, BLOCK), lambda b, g: (0, 0)),
            pl.BlockSpec((None, None, FOURIER_GROUP_DIM, FOURIER_GROUP_DIM), lambda b, g: (layer, g, 0, 0)),
        ],
        out_specs=pl.BlockSpec((seq, LANES), lambda b, g: (b, g)),
        out_shape=jax.ShapeDtypeStruct((batch * seq, FOURIER_WIDTH), BF16),
        scratch_shapes=[pltpu.VMEM((n_outer * _pitch(BLOCK), LANES), F32),
                        pltpu.VMEM((n_outer * _pitch(BLOCK), LANES), F32),
                        pltpu.VMEM((BLOCK * _pitch(n_outer), LANES), F32)],
        compiler_params=pltpu.CompilerParams(
            dimension_semantics=("parallel", "parallel"), vmem_limit_bytes=V7X_VMEM_LIMIT_BYTES),
        name="fourier_mix",
    )(pq, a_mat, tw_cos, tw_sin, c_mat, s_mat, w_lin)


def _outproj_kernel(*refs, x_ranges):
    x_refs = refs[:len(x_ranges)]
    a_ref, f_ref, mod_ref, post_g_ref, fg_ref, w_ref, o_ref, fn_ref, mult_ref = refs[len(x_ranges):]

    def body(x_ref):
        rows = f_ref.shape[0]
        rc = min(ROW_CHUNK, rows)
        for r0 in range(0, rows, rc):
            acc = None
            for c0 in range(0, FOURIER_WIDTH, LANES):
                t = f_ref[r0:r0 + rc, c0:c0 + LANES].astype(F32)
                acc = t * t if acc is None else acc + t * t
            inv = lax.rsqrt(jnp.sum(acc, axis=-1, keepdims=True) * (1.0 / FOURIER_WIDTH) + RMS_EPS)
            for c0 in range(0, FOURIER_WIDTH, LANES):
                sl = slice(c0, c0 + LANES)
                fn_ref[r0:r0 + rc, sl] = (f_ref[r0:r0 + rc, sl].astype(F32) * inv * fg_ref[:, sl]).astype(BF16)
        o_ref[...] = (jnp.dot(a_ref[...], w_ref[:ATTN_WIDTH, :], preferred_element_type=F32)
                      + jnp.dot(fn_ref[...], w_ref[ATTN_WIDTH:, :], preferred_element_type=F32))
        _post_into(o_ref, x_ref, post_g_ref, mod_ref, mult_ref, 1.0)

    _with_owner(x_refs, x_ranges, body)


def _outproj(a, f, x_parts, mod, post_g, f_gain, w_out, *, layer, seq, tm):
    rows, d = sum(p.shape[0] for p in x_parts), x_parts[0].shape[1]
    tiles_per_seq = seq // tm
    x_specs, x_ranges = _part_specs(x_parts, tm)
    return pl.pallas_call(
        functools.partial(_outproj_kernel, x_ranges=x_ranges),
        grid=(rows // tm,),
        in_specs=x_specs + [
            pl.BlockSpec((tm, ATTN_WIDTH), lambda r: (r, 0)),
            pl.BlockSpec((tm, FOURIER_WIDTH), lambda r: (r, 0)),
            pl.BlockSpec((1, N_MOD, d), lambda r: (r // tiles_per_seq, 0, 0)),
            pl.BlockSpec((1, d), lambda r: (0, 0)),
            pl.BlockSpec((1, FOURIER_WIDTH), lambda r: (0, 0)),
            pl.BlockSpec((None, ATTN_WIDTH + FOURIER_WIDTH, d), lambda r: (layer, 0, 0)),
        ],
        out_specs=pl.BlockSpec((tm, d), lambda r: (r, 0)),
        out_shape=jax.ShapeDtypeStruct((rows, d), F32),
        scratch_shapes=[pltpu.VMEM((tm, FOURIER_WIDTH), BF16), pltpu.VMEM((1, d), F32)],
        compiler_params=pltpu.CompilerParams(
            dimension_semantics=("parallel",), vmem_limit_bytes=V7X_VMEM_LIMIT_BYTES),
        name="outproj",
    )(*x_parts, a, f, mod, post_g.reshape(1, d), f_gain.reshape(1, FOURIER_WIDTH), w_out)


def _rope_tables(seq):
    inv_freq = ROPE_THETA ** (-jnp.arange(0, HEAD_DIM, 2, dtype=F32) / HEAD_DIM)
    ang = jnp.arange(seq, dtype=F32)[:, None] * inv_freq[None, :]
    cos, sin = jnp.cos(ang), jnp.sin(ang)
    return jnp.concatenate([cos, cos], axis=-1), jnp.concatenate([-sin, sin], axis=-1)


def _dft_tables(seq):
    n_outer = seq // BLOCK

    def cs(n, scale):
        idx = np.arange(n)
        ang = 2.0 * np.pi * ((idx[:, None] * idx[None, :]) % n) / n
        return np.cos(ang) * scale, np.sin(ang) * scale

    cc, sc = cs(FOURIER_GROUP_DIM, FOURIER_GROUP_DIM ** -0.5)
    wc = np.concatenate([cc, sc], axis=1)
    co, so = cs(n_outer, n_outer ** -0.5)
    a_mat = np.block([[co, -so], [-so, -co]])
    c128, s128 = cs(BLOCK, BLOCK ** -0.5)
    n2 = np.arange(BLOCK)[:, None]
    k1 = np.arange(n_outer)[None, :]
    tw = 2.0 * np.pi * ((n2 * k1) % seq) / seq
    tw_cos = np.broadcast_to(np.cos(tw)[:, :, None], (BLOCK, n_outer, 128))
    tw_sin = np.broadcast_to(np.sin(tw)[:, :, None], (BLOCK, n_outer, 128))
    as_f32 = lambda a: jnp.asarray(np.ascontiguousarray(a), dtype=F32)
    return (as_f32(wc).astype(BF16), as_f32(a_mat).astype(BF16), as_f32(tw_cos), as_f32(tw_sin),
            as_f32(c128).astype(BF16), as_f32(s128).astype(BF16))


def _tile(seq, want):
    return min(seq, want)


def _trunk(x_groups, c_groups, w_mod, b_mod, pre_g, post_g, ffn_w_gate, ffn_w_up, ffn_w_down,
           w_in, attn_sink, fourier_w, branch_g, w_out):
    seq, d = x_groups[0].shape[1:]
    sizes = [x.shape[0] for x in x_groups]
    batch = sum(sizes)
    depth = w_mod.shape[0]
    assert seq % BLOCK == 0 and batch <= MOD_ROWS and all(x.shape[1:] == (seq, d) for x in x_groups)
    n_outer = seq // BLOCK

    c_pad = jnp.zeros((MOD_ROWS, d), F32).at[:batch].set(jnp.concatenate(c_groups, axis=0))
    mod = _modulation(c_pad, w_mod, b_mod).reshape(depth, MOD_ROWS, N_SUBLAYERS, N_MOD, d)

    cos_t, sin_t = _rope_tables(seq)
    wc, a_mat, tw_cos, tw_sin, c128, s128 = _dft_tables(seq)

    wg = ffn_w_gate.astype(BF16)
    wu = ffn_w_up.astype(BF16)
    wd = ffn_w_down.astype(BF16)
    w_in_b = w_in.astype(BF16)
    w_out_b = w_out.astype(BF16)
    w_lin = fourier_w.astype(BF16)

    tm_ffn = _tile(seq, 1024)
    tf = 512 if wg.shape[-1] % 512 == 0 else wg.shape[-1]
    tm_proj = _tile(seq, 512)
    q_blocks = min(16, n_outer)
    tiles_per_seq = seq // tm_ffn
    group_tile0 = [sum(sizes[:i]) * tiles_per_seq for i in range(len(sizes))]

    def ffn(x, l, sub, which, **kw):
        return _ffn(x, mod[l, :, sub], pre_g[l, sub], post_g[l, sub], wg, wu, wd, layer=l, which=which,
                    seq=seq, weight=0.5, tm=tm_ffn, tf=tf, **kw)

    xs = None
    for l in range(depth):
        if l == 0:
            x_parts = [ffn(x.reshape(n * seq, d), l, 0, 0, seq_tile0=t0)
                       for x, n, t0 in zip(x_groups, sizes, group_tile0)]
        else:
            x_parts = [ffn(xs, l, 0, 0)]
        q, k, v, pq = _inproj(x_parts, mod[l, :, 1], pre_g[l, 1], w_in_b, cos_t, sin_t, wc,
                              layer=l, seq=seq, tm=tm_proj)
        a_out = _attention(q, k, v, attn_sink[l], branch_g[l, 0], seq=seq, q_blocks=q_blocks)
        f_raw = _fourier_mix(pq, a_mat, tw_cos, tw_sin, c128, s128, w_lin, layer=l, batch=batch, seq=seq)
        xs = _outproj(a_out, f_raw, x_parts, mod[l, :, 1], post_g[l, 1], branch_g[l, 1], w_out_b,
                      layer=l, seq=seq, tm=tm_proj)
        if l < depth - 1:
            xs = ffn(xs, l, 2, 1)
    outs = []
    for n, t0 in zip(sizes, group_tile0):
        y = ffn(xs, depth - 1, 2, 1, tiles=n * tiles_per_seq, in_tile0=t0, seq_tile0=t0)
        outs.append(y.reshape(n, seq, d))
    return outs


def kernel(x_prompt, x_sample, c_prompt, c_sample, w_mod, b_mod, pre_g, post_g, ffn_w_gate, ffn_w_up,
           ffn_w_down, w_in, attn_sink, fourier_w, branch_g, w_out):
    y_prompt, y_sample = _trunk([x_prompt, x_sample], [c_prompt, c_sample], w_mod, b_mod, pre_g, post_g,
                                ffn_w_gate, ffn_w_up, ffn_w_down, w_in, attn_sink, fourier_w, branch_g, w_out)
    return y_prompt, y_sample
```
